```python
import math
import jax, jax.numpy as jnp
from jax import lax
import numpy as np

D_MODEL = 2048
BATCH = 32
SEQ = 256
DEPTH = 2
DEC_BATCH = 8
DEC_SEQ = 1024
PAST_LEN = 512

GRID_W = 64
N_FNO_LAYERS = (DEPTH + 1) // 2
N_SSD_LAYERS = DEPTH // 2
FNO_WIDTH = 2 * D_MODEL
FNO_GROUPS = 16
FNO_GROUP_DIM = FNO_WIDTH // FNO_GROUPS
SSD_WIDTH = 2 * D_MODEL
SSD_HEAD_DIM = 64
SSD_HEADS = SSD_WIDTH // SSD_HEAD_DIM
SSD_GROUPS = 8
SSD_D_STATE = 128
SSD_CONV_W = 5
SSD_CONV_DIM = SSD_WIDTH + 2 * SSD_GROUPS * SSD_D_STATE
SSD_IN_DIM = SSD_WIDTH + SSD_CONV_DIM + 2 * SSD_HEADS
CHUNK = 128
DEEPNORM_ALPHA = (2 * DEPTH) ** 0.25
DEEPNORM_BETA = (8 * DEPTH) ** -0.25
LN_EPS = 1e-5

kernel_name = "fnet_ssd_hybrid_diffusion_step"


def _layer_norm(x, g=None, b=None):
    xf = x.astype(jnp.float32)
    mu = jnp.mean(xf, axis=-1, keepdims=True)
    var = jnp.mean(jnp.square(xf - mu), axis=-1, keepdims=True)
    y = (xf - mu) * lax.rsqrt(var + LN_EPS)
    if g is not None:
        y = y * g.astype(jnp.float32) + b.astype(jnp.float32)
    return y.astype(x.dtype)


def _rms_norm(x, w):
    xf = x.astype(jnp.float32)
    y = xf * lax.rsqrt(jnp.mean(jnp.square(xf), axis=-1, keepdims=True) + LN_EPS)
    return (y * w.astype(jnp.float32)).astype(x.dtype)


def _sincos_1d(pos, dim):
    omega = 1.0 / (10000.0 ** (jnp.arange(dim // 2, dtype=jnp.float32) / (dim / 2)))
    ang = pos.astype(jnp.float32)[:, None] * omega[None, :]
    return jnp.concatenate([jnp.sin(ang), jnp.cos(ang)], axis=-1)


def _grid_pos_embed(n_tokens, dim):
    t = jnp.arange(n_tokens)
    row, col = t // GRID_W, t % GRID_W
    return jnp.concatenate([_sincos_1d(row, dim // 2), _sincos_1d(col, dim // 2)], axis=-1)


def _centred_dwconv(u, w, bias):
    pad = SSD_CONV_W // 2
    L = u.shape[1]
    up = jnp.pad(u, ((0, 0), (pad, pad), (0, 0)))
    acc = bias
    for k in range(SSD_CONV_W):
        acc = acc + up[:, k:k + L] * w[k]
    return acc


def _segsum_exp(a_cs):
    T = a_cs.shape[-1]
    diff = a_cs[..., :, None] - a_cs[..., None, :]
    mask = jnp.tril(jnp.ones((T, T), dtype=bool))
    return jnp.where(mask, jnp.exp(jnp.where(mask, diff, 0.0)), 0.0)


def _ssd_scan(x, dt, a, bm, cm, init):
    b, L, H, P = x.shape
    G, N = bm.shape[2], bm.shape[3]
    R = H // G
    nc = L // CHUNK
    x = x.astype(jnp.float32)
    dt = dt.astype(jnp.float32)
    xdt = (x * dt[..., None]).reshape(b, nc, CHUNK, G, R, P)
    da = (dt * a.astype(jnp.float32)).reshape(b, nc, CHUNK, G, R).transpose(0, 3, 4, 1, 2)
    bm = bm.astype(jnp.float32).reshape(b, nc, CHUNK, G, N)
    cm = cm.astype(jnp.float32).reshape(b, nc, CHUNK, G, N)
    a_cs = jnp.cumsum(da, axis=-1)
    lmat = _segsum_exp(a_cs)
    y_diag = jnp.einsum('bclgn,bcsgn,bgrcls,bcsgrp->bclgrp', cm, bm, lmat, xdt)
    decay_states = jnp.exp(a_cs[..., -1:] - a_cs)
    states = jnp.einsum('bclgn,bgrcl,bclgrp->bcgrpn', bm, decay_states, xdt)
    init = init.astype(jnp.float32).reshape(b, 1, G, R, P, N)
    states = jnp.concatenate([init, states], axis=1)
    chunk_tot = jnp.pad(a_cs[..., -1], ((0, 0), (0, 0), (0, 0), (1, 0)))
    decay_chunk = _segsum_exp(jnp.cumsum(chunk_tot, axis=-1))
    new_states = jnp.einsum('bgrzc,bcgrpn->bzgrpn', decay_chunk, states)
    states_in, final = new_states[:, :-1], new_states[:, -1]
    y_off = jnp.einsum('bclgn,bcgrpn,bgrcl->bclgrp', cm, states_in, jnp.exp(a_cs))
    y = (y_diag + y_off).reshape(b, L, H, P)
    return y, final.reshape(b, H, P, N)


def _fourier_mixer(h, w_in, w_out):
    b, L, _ = h.shape
    uz = h @ w_in
    u, z = uz[..., :FNO_WIDTH], uz[..., FNO_WIDTH:]
    u = u.reshape(b, L, FNO_GROUPS, FNO_GROUP_DIM).astype(jnp.float32)
    y = jnp.fft.fftn(u, axes=(1, 3), norm='ortho').real
    y = y.reshape(b, L, FNO_WIDTH).astype(h.dtype) * jax.nn.silu(z)
    return y @ w_out


def _ssd_mixer(h, init, w_in, conv_w, conv_b, dt_bias, a_log, d_skip, norm_w, w_out):
    b, L, _ = h.shape
    proj = h @ w_in
    z = proj[..., :SSD_WIDTH]
    xbc = proj[..., SSD_WIDTH:SSD_WIDTH + SSD_CONV_DIM]
    dt_raw = proj[..., SSD_WIDTH + SSD_CONV_DIM:].reshape(b, L, 2, SSD_HEADS)
    xbc = jax.nn.silu(_centred_dwconv(xbc, conv_w, conv_b))
    gn = SSD_GROUPS * SSD_D_STATE
    xs = xbc[..., :SSD_WIDTH].reshape(b, L, SSD_HEADS, SSD_HEAD_DIM)
    bm = xbc[..., SSD_WIDTH:SSD_WIDTH + gn].reshape(b, L, SSD_GROUPS, SSD_D_STATE)
    cm = xbc[..., SSD_WIDTH + gn:].reshape(b, L, SSD_GROUPS, SSD_D_STATE)
    dt = jax.nn.softplus(dt_raw.astype(jnp.float32) + dt_bias.astype(jnp.float32))
    a = -jnp.exp(a_log.astype(jnp.float32))
    y_f, s_f = _ssd_scan(xs, dt[:, :, 0], a[0], bm, cm, init[:, 0])
    flip = lambda t: jnp.flip(t, axis=1)
    y_b, s_b = _ssd_scan(flip(xs), flip(dt[:, :, 1]), a[1], flip(bm), flip(cm), init[:, 1])
    y = y_f + flip(y_b) + d_skip.astype(jnp.float32)[:, None] * xs.astype(jnp.float32)
    y = y.reshape(b, L, SSD_WIDTH).astype(h.dtype) * jax.nn.silu(z)
    y = _rms_norm(y, norm_w)
    return y @ w_out, jnp.stack([s_f, s_b], axis=1)


def _trunk(x, cond, init_states, w_ada, b_ada, ln_g, ln_b, fno_w_in, fno_w_out,
           ssd_w_in, ssd_conv_w, ssd_conv_b, ssd_dt_bias, ssd_a_log, ssd_d, ssd_norm_w, ssd_w_out):
    finals = []
    for i in range(DEPTH):
        mod = jax.nn.silu(cond) @ w_ada[i] + b_ada[i]
        shift = mod[:, None, :D_MODEL]
        scale = mod[:, None, D_MODEL:2 * D_MODEL]
        gate = mod[:, None, 2 * D_MODEL:]
        h = _layer_norm(x) * (1.0 + scale) + shift
        j = i // 2
        if i % 2 == 0:
            out = _fourier_mixer(h, fno_w_in[j], fno_w_out[j])
        else:
            out, fin = _ssd_mixer(h, init_states[:, j], ssd_w_in[j], ssd_conv_w[j], ssd_conv_b[j],
                                  ssd_dt_bias[j], ssd_a_log[j], ssd_d[j], ssd_norm_w[j], ssd_w_out[j])
            finals.append(fin)
        x = _layer_norm(DEEPNORM_ALPHA * x + gate * out, ln_g[i], ln_b[i])
    return x, finals


def setup_inputs(seed: int = 0) -> dict:
    key = jax.random.key(seed)
    ks = jax.random.split(key, 24)
    f32 = jnp.float32
    nrm = lambda k, s, sc: jax.random.normal(k, s, f32) * sc
    dt0 = jnp.exp(jax.random.uniform(ks[14], (N_SSD_LAYERS, 2, SSD_HEADS), f32,
                                     math.log(1e-3), math.log(1e-1)))
    return {
        "x_prompt": nrm(ks[0], (BATCH, SEQ, D_MODEL), 1.0),
        "x_sample": nrm(ks[1], (DEC_BATCH, DEC_SEQ, D_MODEL), 1.0),
        "state_ssd_ctx": nrm(ks[2], (DEC_BATCH, N_SSD_LAYERS, 2, SSD_HEADS, SSD_HEAD_DIM, SSD_D_STATE), 0.1),
        "c": nrm(ks[3], (DEC_BATCH, D_MODEL), 1.0),
        "c_ctx": nrm(ks[4], (D_MODEL,), 1.0),
        "w_ada": nrm(ks[5], (DEPTH, D_MODEL, 3 * D_MODEL), 0.5 * D_MODEL ** -0.5),
        "b_ada": nrm(ks[6], (DEPTH, 3 * D_MODEL), 0.02),
        "ln_g": 1.0 + nrm(ks[7], (DEPTH, D_MODEL), 0.02),
        "ln_b": nrm(ks[8], (DEPTH, D_MODEL), 0.02),
        "fno_w_in": nrm(ks[9], (N_FNO_LAYERS, D_MODEL, 2 * FNO_WIDTH), D_MODEL ** -0.5),
        "fno_w_out": nrm(ks[10], (N_FNO_LAYERS, FNO_WIDTH, D_MODEL), DEEPNORM_BETA * FNO_WIDTH ** -0.5),
        "ssd_w_in": nrm(ks[11], (N_SSD_LAYERS, D_MODEL, SSD_IN_DIM), D_MODEL ** -0.5),
        "ssd_conv_w": nrm(ks[12], (N_SSD_LAYERS, SSD_CONV_W, SSD_CONV_DIM), SSD_CONV_W ** -0.5),
        "ssd_conv_b": nrm(ks[13], (N_SSD_LAYERS, SSD_CONV_DIM), 0.02),
        "ssd_dt_bias": dt0 + jnp.log(-jnp.expm1(-dt0)),
        "ssd_a_log": jnp.log(jax.random.uniform(ks[15], (N_SSD_LAYERS, 2, SSD_HEADS), f32, 1.0, 16.0)),
        "ssd_d": 1.0 + nrm(ks[16], (N_SSD_LAYERS, SSD_HEADS), 0.1),
        "ssd_norm_w": 1.0 + nrm(ks[17], (N_SSD_LAYERS, SSD_WIDTH), 0.02),
        "ssd_w_out": nrm(ks[18], (N_SSD_LAYERS, SSD_WIDTH, D_MODEL), DEEPNORM_BETA * SSD_WIDTH ** -0.5),
    }


def reference(x_prompt, x_sample, state_ssd_ctx, c, c_ctx, w_ada, b_ada, ln_g, ln_b,
              fno_w_in, fno_w_out, ssd_w_in, ssd_conv_w, ssd_conv_b, ssd_dt_bias, ssd_a_log,
              ssd_d, ssd_norm_w, ssd_w_out):
    weights = (w_ada, b_ada, ln_g, ln_b, fno_w_in, fno_w_out, ssd_w_in, ssd_conv_w, ssd_conv_b,
               ssd_dt_bias, ssd_a_log, ssd_d, ssd_norm_w, ssd_w_out)
    zero_init = jnp.zeros((x_prompt.shape[0], N_SSD_LAYERS, 2, SSD_HEADS, SSD_HEAD_DIM, SSD_D_STATE),
                          dtype=jnp.float32)
    y_prompt, finals = _trunk(x_prompt, c_ctx[None, :], zero_init, *weights)
    state_ssd = jnp.stack(finals, axis=1).astype(x_prompt.dtype)
    n_lat = x_sample.shape[1]
    pos = _grid_pos_embed(n_lat, D_MODEL).astype(x_sample.dtype)
    y_sample, _ = _trunk(x_sample + pos[None], c, state_ssd_ctx, *weights)
    return (y_prompt, y_sample, state_ssd)
```

```python
import functools
import math

import numpy as np
import jax
import jax.numpy as jnp
from jax import lax
from jax.experimental import pallas as pl
from jax.experimental.pallas import tpu as pltpu

F32 = jnp.float32
BF16 = jnp.bfloat16

DEPTH = 2
GRID_W = 64
FNO_GROUP_DIM = 256
SSD_HEAD_DIM = 64
SSD_GROUPS = 8
SSD_D_STATE = 128
SSD_CONV_W = 5
CHUNK = 128
DEEPNORM_ALPHA = (2 * DEPTH) ** 0.25
LN_EPS = 1e-5
COND_ROWS = 16
CTX_ROW = 8
LANES = 128
MIB = 1024 * 1024


def _params(semantics, vmem_mib):
    return pltpu.CompilerParams(dimension_semantics=semantics, vmem_limit_bytes=vmem_mib * MIB)


def _silu(x):
    return x * jax.nn.sigmoid(x)


def _dot(a, b):
    return jnp.dot(a, b, preferred_element_type=F32)


def _mod_kernel(c_ref, w_ref, b_ref, o_ref):
    s = _silu(c_ref[...]).astype(BF16)
    o_ref[...] = _dot(s, w_ref[...].astype(BF16)) + b_ref[...]


def _modulation(cond, w_ada, b_ada):
    depth, d, n = w_ada.shape
    tn = 512
    return pl.pallas_call(
        _mod_kernel,
        grid=(depth, n // tn),
        in_specs=[
            pl.BlockSpec((COND_ROWS, d), lambda i, j: (0, 0)),
            pl.BlockSpec((None, d, tn), lambda i, j: (i, 0, j)),
            pl.BlockSpec((None, 1, tn), lambda i, j: (i, 0, j)),
        ],
        out_specs=pl.BlockSpec((None, COND_ROWS, tn), lambda i, j: (i, 0, j)),
        out_shape=jax.ShapeDtypeStruct((depth, COND_ROWS, n), F32),
        compiler_params=_params(("parallel", "parallel"), 32),
        name="adaln_mod",
    )(cond, w_ada, b_ada.reshape(depth, 1, n))


def _ln_modulate(x, shift, scale):
    mu = jnp.mean(x, axis=-1, keepdims=True)
    xc = x - mu
    var = jnp.mean(xc * xc, axis=-1, keepdims=True)
    return xc * lax.rsqrt(var + LN_EPS) * (1.0 + scale) + shift


def _ln_mm_kernel(*refs, has_pos, has_dt):
    refs = list(refs)
    x_ref = refs.pop(0)
    pos_ref = refs.pop(0) if has_pos else None
    shift_ref, scale_ref, w_ref = refs.pop(0), refs.pop(0), refs.pop(0)
    wdt_ref = refs.pop(0) if has_dt else None
    o_ref = refs.pop(0)
    dt_ref = refs.pop(0) if has_dt else None
    h_scr = refs.pop(0)

    @pl.when(pl.program_id(1) == 0)
    def _():
        x = x_ref[...]
        if has_pos:
            x = x + pos_ref[...]
        h = _ln_modulate(x, shift_ref[...], scale_ref[...]).astype(BF16)
        h_scr[...] = h
        if has_dt:
            dt_ref[...] = _dot(h, wdt_ref[...])

    o_ref[...] = _dot(h_scr[...], w_ref[...]).astype(o_ref.dtype)


def _ln_in_proj(x, pos, mod5, layer, cond_row, w, w_dt, *, tm, tn):
    t, d = x.shape
    n = w.shape[1]
    has_pos, has_dt = pos is not None, w_dt is not None
    in_specs = [pl.BlockSpec((tm, d), lambda i, j: (i, 0))]
    args = [x]
    if has_pos:
        pos_blocks = pos.shape[0] // tm
        in_specs.append(pl.BlockSpec((tm, d), lambda i, j: (i % pos_blocks, 0)))
        args.append(pos)
    for part in (0, 1):
        in_specs.append(pl.BlockSpec((None, None, None, 1, d),
                                     lambda i, j, part=part: (layer, cond_row(i), part, 0, 0)))
        args.append(mod5)
    in_specs.append(pl.BlockSpec((d, tn), lambda i, j: (0, j)))
    args.append(w)
    out_specs = [pl.BlockSpec((tm, tn), lambda i, j: (i, j))]
    out_shape = [jax.ShapeDtypeStruct((t, n), BF16)]
    if has_dt:
        in_specs.append(pl.BlockSpec((d, LANES), lambda i, j: (0, 0)))
        args.append(w_dt)
        out_specs.append(pl.BlockSpec((tm, LANES), lambda i, j: (i, 0)))
        out_shape.append(jax.ShapeDtypeStruct((t, LANES), F32))
    out = pl.pallas_call(
        functools.partial(_ln_mm_kernel, has_pos=has_pos, has_dt=has_dt),
        grid=(t // tm, n // tn),
        in_specs=in_specs,
        out_specs=out_specs,
        out_shape=out_shape,
        scratch_shapes=[pltpu.VMEM((tm, d), BF16)],
        compiler_params=_params(("parallel", "arbitrary"), 48),
        name="ln_in_proj",
    )(*args)
    return out if has_dt else out[0]


def _dft_matrices(seq, group_dim):
    def cos_sin(n):
        k = np.arange(n, dtype=np.int64)
        ang = 2.0 * np.pi * ((k[:, None] * k[None, :]) % n).astype(np.float64) / n
        return np.cos(ang), np.sin(ang)
    cl, sl = cos_sin(seq)
    cc, sc = cos_sin(group_dim)
    norm = 1.0 / math.sqrt(seq * group_dim)
    chan = np.concatenate([cc, sc], axis=1)
    posm = np.concatenate([cl * norm, -sl * norm], axis=1)
    return jnp.asarray(chan, dtype=F32).astype(BF16), jnp.asarray(posm, dtype=F32).astype(BF16)


def _fno_kernel(u_ref, z_ref, chan_ref, posm_ref, o_ref):
    seq, tw = u_ref.shape
    cg = FNO_GROUP_DIM
    for k in range(tw // cg):
        cols = slice(k * cg, (k + 1) * cg)
        t = _dot(u_ref[:, cols], chan_ref[...])
        y = (_dot(posm_ref[:, :seq], t[:, :cg].astype(BF16))
             + _dot(posm_ref[:, seq:], t[:, cg:].astype(BF16)))
        o_ref[:, cols] = (y * _silu(z_ref[:, cols].astype(F32))).astype(o_ref.dtype)


def _fno_core(uz, batch, seq, width, *, tw):
    chan, posm = _dft_matrices(seq, FNO_GROUP_DIM)
    nw = width // tw
    return pl.pallas_call(
        _fno_kernel,
        grid=(batch, nw),
        in_specs=[
            pl.BlockSpec((seq, tw), lambda b, j: (b, j)),
            pl.BlockSpec((seq, tw), lambda b, j: (b, nw + j)),
            pl.BlockSpec(chan.shape, lambda b, j: (0, 0)),
            pl.BlockSpec(posm.shape, lambda b, j: (0, 0)),
        ],
        out_specs=pl.BlockSpec((seq, tw), lambda b, j: (b, j)),
        out_shape=jax.ShapeDtypeStruct((batch * seq, width), BF16),
        compiler_params=_params(("parallel", "parallel"), 48),
        name="fno_core",
    )(uz, uz, chan, posm)


def _out_ln_kernel(*refs, has_pos, gated):
    refs = list(refs)
    a_ref = refs.pop(0)
    z_ref, nw_ref = (refs.pop(0), refs.pop(0)) if gated else (None, None)
    w_ref, x_ref = refs.pop(0), refs.pop(0)
    pos_ref = refs.pop(0) if has_pos else None
    gate_ref, g_ref, b_ref, o_ref = refs

    if gated:
        y = a_ref[...] * _silu(z_ref[...].astype(F32))
        ms = jnp.mean(y * y, axis=-1, keepdims=True)
        a = (y * lax.rsqrt(ms + LN_EPS) * nw_ref[...]).astype(BF16)
    else:
        a = a_ref[...]
    x = x_ref[...]
    if has_pos:
        x = x + pos_ref[...]
    r = DEEPNORM_ALPHA * x + gate_ref[...] * _dot(a, w_ref[...])
    mu = jnp.mean(r, axis=-1, keepdims=True)
    rc = r - mu
    var = jnp.mean(rc * rc, axis=-1, keepdims=True)
    o_ref[...] = rc * lax.rsqrt(var + LN_EPS) * g_ref[...] + b_ref[...]


def _out_proj_ln(a, z, norm_w, w, x, pos, mod5, layer, cond_row, ln_g, ln_b, *, tm):
    t, kdim = a.shape
    d = w.shape[1]
    has_pos, gated = pos is not None, z is not None
    row_k = pl.BlockSpec((tm, kdim), lambda i: (i, 0))
    row_d = pl.BlockSpec((tm, d), lambda i: (i, 0))
    in_specs, args = [row_k], [a]
    if gated:
        in_specs += [row_k, pl.BlockSpec((1, kdim), lambda i: (0, 0))]
        args += [z, norm_w.reshape(1, kdim)]
    in_specs += [pl.BlockSpec((kdim, d), lambda i: (0, 0), pipeline_mode=pl.Buffered(1)), row_d]
    args += [w, x]
    if has_pos:
        pos_blocks = pos.shape[0] // tm
        in_specs.append(pl.BlockSpec((tm, d), lambda i: (i % pos_blocks, 0)))
        args.append(pos)
    in_specs.append(pl.BlockSpec((None, None, None, 1, d), lambda i: (layer, cond_row(i), 2, 0, 0)))
    args.append(mod5)
    for v in (ln_g, ln_b):
        in_specs.append(pl.BlockSpec((None, 1, d), lambda i: (layer, 0, 0)))
        args.append(v.reshape(v.shape[0], 1, d))
    return pl.pallas_call(
        functools.partial(_out_ln_kernel, has_pos=has_pos, gated=gated),
        grid=(t // tm,),
        in_specs=in_specs,
        out_specs=row_d,
        out_shape=jax.ShapeDtypeStruct((t, d), F32),
        compiler_params=_params(("parallel",), 56),
        name="out_proj_ln",
    )(*args)


def _conv_kernel(x_ref, w_ref, b_ref, o_ref, pad_scr):
    seq, tc = x_ref.shape
    halo = 8
    pad_scr[0:halo, :] = jnp.zeros((halo, tc), F32)
    pad_scr[halo + seq:2 * halo + seq, :] = jnp.zeros((halo, tc), F32)
    pad_scr[halo:halo + seq, :] = x_ref[...].astype(F32)
    padded = pad_scr[...]
    acc = jnp.broadcast_to(b_ref[...], (seq, tc))
    for k in range(SSD_CONV_W):
        shift = (SSD_CONV_W // 2 - k) % (seq + 2 * halo)
        tap = padded if shift == 0 else pltpu.roll(padded, shift, axis=0)
        acc = acc + tap[halo:halo + seq, :] * w_ref[k:k + 1, :]
    o_ref[...] = _silu(acc).astype(o_ref.dtype)


def _conv_silu(zx, conv_w, conv_b, batch, seq, col0, *, tc):
    conv_dim = conv_w.shape[1]
    c0 = col0 // tc
    w8 = jnp.zeros((8, conv_dim), F32).at[:SSD_CONV_W].set(conv_w)
    return pl.pallas_call(
        _conv_kernel,
        grid=(batch, conv_dim // tc),
        in_specs=[pl.BlockSpec((seq, tc), lambda b, j: (b, c0 + j)),
                  pl.BlockSpec((8, tc), lambda b, j: (0, j)),
                  pl.BlockSpec((1, tc), lambda b, j: (0, j))],
        out_specs=pl.BlockSpec((seq, tc), lambda b, j: (b, j)),
        out_shape=jax.ShapeDtypeStruct((batch * seq, conv_dim), BF16),
        scratch_shapes=[pltpu.VMEM((seq + 16, tc), F32)],
        compiler_params=_params(("parallel", "parallel"), 48),
        name="ssd_conv_silu",
    )(zx, w8, conv_b.reshape(1, conv_dim))


def _split3(x):
    hi = x.astype(BF16)
    r = x - hi.astype(F32)
    mid = r.astype(BF16)
    lo = (r - mid.astype(F32)).astype(BF16)
    return hi, mid, lo


def _dt_kernel(raw_ref, bias_ref, alog_ref, dt_ref, csc_ref, csr_ref):
    n_ch = csr_ref.shape[0]
    row = lax.broadcasted_iota(jnp.int32, (CHUNK, CHUNK), 0)
    col = lax.broadcasted_iota(jnp.int32, (CHUNK, CHUNK), 1)
    tri_f = jnp.where(row >= col, 1.0, 0.0).astype(BF16)
    tri_b = jnp.where(row <= col, 1.0, 0.0).astype(BF16)
    is_fwd = col < (LANES // 2)
    a = -jnp.exp(alog_ref[...])
    for k in range(n_ch):
        rows = slice(k * CHUNK, (k + 1) * CHUNK)
        x = raw_ref[rows, :] + bias_ref[...]
        dt = jnp.maximum(x, 0.0) + jnp.log1p(jnp.exp(-jnp.abs(x)))
        parts = _split3(dt * a)
        cs_f = sum(_dot(tri_f, p) for p in parts)
        cs_b = sum(_dot(tri_b, p) for p in parts)
        cs = jnp.where(is_fwd, cs_f, cs_b)
        dt_ref[rows, :] = dt
        csc_ref[rows, :] = cs
        csr_ref[k] = cs.T


def _dt_prep(dt_raw, dt_bias, a_log, *, n_ch):
    t = dt_raw.shape[0]
    tm = n_ch * CHUNK
    row_spec = pl.BlockSpec((tm, LANES), lambda i: (i, 0))
    vec_spec = pl.BlockSpec((1, LANES), lambda i: (0, 0))
    return pl.pallas_call(
        _dt_kernel,
        grid=(t // tm,),
        in_specs=[row_spec, vec_spec, vec_spec],
        out_specs=[row_spec, row_spec, pl.BlockSpec((n_ch, CHUNK, LANES), lambda i: (i, 0, 0))],
        out_shape=[jax.ShapeDtypeStruct((t, LANES), F32),
                   jax.ShapeDtypeStruct((t, LANES), F32),
                   jax.ShapeDtypeStruct((t // CHUNK, CHUNK, LANES), F32)],
        compiler_params=_params(("parallel",), 32),
        name="ssd_dt_prep",
    )(dt_raw, dt_bias.reshape(1, LANES), a_log.reshape(1, LANES))


HEADS_PER_GROUP = 8
PAIRS = HEADS_PER_GROUP // 2
GROUP_W = HEADS_PER_GROUP * SSD_HEAD_DIM


def _ssd_kernel(*refs, nc, has_init, want_final):
    refs = list(refs)
    xs_ref, b_ref, c_ref, dt_ref, csc_ref, csr_ref, dsk_ref = refs[:7]
    del refs[:7]
    init_ref = refs.pop(0) if has_init else None
    y_ref = refs.pop(0)
    fin_ref = refs.pop(0) if want_final else None
    st_scr = refs.pop(0)

    p = SSD_HEAD_DIM
    lane = lax.broadcasted_iota(jnp.int32, (CHUNK, LANES), 1)
    row = lax.broadcasted_iota(jnp.int32, (CHUNK, CHUNK), 0)
    col = lax.broadcasted_iota(jnp.int32, (CHUNK, CHUNK), 1)
    lo_half = lane < p

    def pair_bcast(v0, v1):
        return jnp.where(lo_half, v0, v1)

    for d in (0, 1):
        causal = (row >= col) if d == 0 else (row <= col)
        for q in range(PAIRS):
            lanes = slice(q * LANES, (q + 1) * LANES)
            if has_init:
                blk = init_ref[d, 2 * q:2 * q + 2].reshape(2 * p, SSD_D_STATE)
                st_scr[:, lanes] = blk.T
            else:
                st_scr[:, lanes] = jnp.zeros((SSD_D_STATE, LANES), F32)

        def chunk_body(i, carry, d=d, causal=causal):
            c = i if d == 0 else nc - 1 - i
            r0 = pl.multiple_of(c * CHUNK, CHUNK)
            rows = pl.ds(r0, CHUNK)
            bc = b_ref[rows, :]
            cc = c_ref[rows, :]
            g = lax.dot_general(cc, bc, (((1,), (1,)), ((), ())), preferred_element_type=F32)
            bt = bc.astype(F32).T.astype(BF16)
            dtc = dt_ref[rows, :]
            csc = csc_ref[rows, :]
            csr = csr_ref[c]
            edge = CHUNK - 1 if d == 0 else 0
            y_off = _dot(cc, st_scr[...].astype(BF16))
            for q in range(PAIRS):
                lanes = slice(q * LANES, (q + 1) * LANES)
                j0 = d * HEADS_PER_GROUP + 2 * q
                x = xs_ref[rows, lanes].astype(F32)
                xdt = x * pair_bcast(dtc[:, j0:j0 + 1], dtc[:, j0 + 1:j0 + 2])
                xdt_b = xdt.astype(BF16)
                y_h = []
                for j in (j0, j0 + 1):
                    diff = csc[:, j:j + 1] - csr[j:j + 1, :]
                    lmat = jnp.where(causal, jnp.exp(jnp.minimum(diff, 0.0)), 0.0)
                    y_h.append(_dot((g * lmat).astype(BF16), xdt_b))
                cs_q = pair_bcast(csc[:, j0:j0 + 1], csc[:, j0 + 1:j0 + 2])
                tot_q = pair_bcast(csc[edge:edge + 1, j0:j0 + 1], csc[edge:edge + 1, j0 + 1:j0 + 2])
                y = pair_bcast(y_h[0], y_h[1]) + y_off[:, lanes] * jnp.exp(cs_q)
                if d == 0:
                    y_ref[rows, lanes] = y + dsk_ref[:, lanes] * x
                else:
                    y_ref[rows, lanes] += y
                xdec = (xdt * jnp.exp(tot_q - cs_q)).astype(BF16)
                st_scr[:, lanes] = st_scr[:, lanes] * jnp.exp(tot_q) + _dot(bt, xdec)
            return carry

        lax.fori_loop(0, nc, chunk_body, 0)

        if want_final:
            for q in range(PAIRS):
                lanes = slice(q * LANES, (q + 1) * LANES)
                fin_ref[d, 2 * q:2 * q + 2] = st_scr[:, lanes].T.reshape(2, p, SSD_D_STATE)


def _ssd_scan(xbc, dt_g, csc_g, csr_g, d_skip, init, batch, seq, *, want_final):
    nc = seq // CHUNK
    width = SSD_GROUPS * GROUP_W
    heads = SSD_GROUPS * HEADS_PER_GROUP
    has_init = init is not None
    in_specs = [
        pl.BlockSpec((seq, GROUP_W), lambda b, g: (b, g)),
        pl.BlockSpec((seq, SSD_D_STATE), lambda b, g: (b, width // SSD_D_STATE + g)),
        pl.BlockSpec((seq, SSD_D_STATE), lambda b, g: (b, width // SSD_D_STATE + SSD_GROUPS + g)),
        pl.BlockSpec((None, seq, 16), lambda b, g: (g, b, 0)),
        pl.BlockSpec((None, seq, 16), lambda b, g: (g, b, 0)),
        pl.BlockSpec((None, nc, 16, CHUNK), lambda b, g: (g, b, 0, 0)),
        pl.BlockSpec((1, GROUP_W), lambda b, g: (0, g)),
    ]
    args = [xbc, xbc, xbc, dt_g, csc_g, csr_g, d_skip]
    state_spec = pl.BlockSpec((None, 2, HEADS_PER_GROUP, SSD_HEAD_DIM, SSD_D_STATE),
                              lambda b, g: (b, 0, g, 0, 0))
    if has_init:
        in_specs.append(state_spec)
        args.append(init)
    out_specs = [pl.BlockSpec((seq, GROUP_W), lambda b, g: (b, g))]
    out_shape = [jax.ShapeDtypeStruct((batch * seq, width), F32)]
    if want_final:
        out_specs.append(state_spec)
        out_shape.append(jax.ShapeDtypeStruct((batch, 2, heads, SSD_HEAD_DIM, SSD_D_STATE), F32))
    out = pl.pallas_call(
        functools.partial(_ssd_kernel, nc=nc, has_init=has_init, want_final=want_final),
        grid=(batch, SSD_GROUPS),
        in_specs=in_specs,
        out_specs=out_specs,
        out_shape=out_shape,
        scratch_shapes=[pltpu.VMEM((SSD_D_STATE, GROUP_W), F32)],
        compiler_params=_params(("parallel", "parallel"), 48),
        name="ssd_scan",
    )(*args)
    return out if want_final else (out[0], None)


def _group_heads(v):
    t = v.shape[0]
    v = v.reshape(t, 2, SSD_GROUPS, HEADS_PER_GROUP)
    return v.transpose(2, 0, 1, 3).reshape(SSD_GROUPS, t, 2 * HEADS_PER_GROUP)


def _group_heads_rows(v):
    n = v.shape[0]
    v = v.reshape(n, 2, SSD_GROUPS, HEADS_PER_GROUP, CHUNK)
    return v.transpose(2, 0, 1, 3, 4).reshape(SSD_GROUPS, n, 2 * HEADS_PER_GROUP, CHUNK)


def _trunk(x, pos, batch, seq, per_seq_cond, init, mod5, wts, *, want_final):
    (ln_g, ln_b, fno_w_in, fno_w_out, ssd_w_zx, ssd_w_dt, conv_w, conv_b, dt_bias, a_log,
     d_skip, norm_w, ssd_w_out) = wts
    fno_width = fno_w_out.shape[0]
    ssd_width = ssd_w_out.shape[0]
    tm_in, tm_out = 512, 256

    def cond_row(tm):
        if not per_seq_cond:
            return lambda i: CTX_ROW
        assert seq % tm == 0
        return lambda i: (i * tm) // seq

    uz = _ln_in_proj(x, pos, mod5, 0, cond_row(tm_in), fno_w_in, None, tm=tm_in, tn=2048)
    yg = _fno_core(uz, batch, seq, fno_width, tw=min(fno_width, 4096 * 256 // seq))
    x1 = _out_proj_ln(yg, None, None, fno_w_out, x, pos, mod5, 0, cond_row(tm_out), ln_g, ln_b, tm=tm_out)

    zx, dt_raw = _ln_in_proj(x1, None, mod5, 1, cond_row(tm_in), ssd_w_zx, ssd_w_dt, tm=tm_in, tn=2048)
    xbc = _conv_silu(zx, conv_w, conv_b, batch, seq, ssd_width, tc=1024)
    dt, csc, csr = _dt_prep(dt_raw, dt_bias, a_log, n_ch=8)
    y, fin = _ssd_scan(xbc, _group_heads(dt), _group_heads(csc), _group_heads_rows(csr),
                       d_skip, init, batch, seq, want_final=want_final)
    x2 = _out_proj_ln(y, zx, norm_w, ssd_w_out, x1, None, mod5, 1, cond_row(tm_out), ln_g, ln_b, tm=tm_out)
    return x2, fin


def _sincos(pos, dim):
    omega = 1.0 / (10000.0 ** (np.arange(dim // 2, dtype=np.float64) / (dim / 2)))
    ang = pos.astype(np.float64)[:, None] * omega[None, :]
    return np.concatenate([np.sin(ang), np.cos(ang)], axis=-1)


def _grid_pos_embed(n_tokens, dim):
    t = np.arange(n_tokens)
    return np.concatenate([_sincos(t // GRID_W, dim // 2), _sincos(t % GRID_W, dim // 2)], axis=-1)


def kernel(x_prompt, x_sample, state_ssd_ctx, c, c_ctx, w_ada, b_ada, ln_g, ln_b, fno_w_in, fno_w_out,
           ssd_w_in, ssd_conv_w, ssd_conv_b, ssd_dt_bias, ssd_a_log, ssd_d, ssd_norm_w, ssd_w_out):
    batch, seq, d = x_prompt.shape
    dec_batch, dec_seq, _ = x_sample.shape
    ssd_width = ssd_w_out.shape[1]
    conv_dim = ssd_conv_w.shape[2]

    cond = jnp.zeros((COND_ROWS, d), F32).at[:dec_batch].set(c).at[CTX_ROW].set(c_ctx)
    mod = _modulation(cond, w_ada, b_ada)
    mod5 = mod.reshape(DEPTH, COND_ROWS, 3, 1, d)

    w_in = ssd_w_in[0]
    wts = (ln_g, ln_b,
           fno_w_in[0].astype(BF16), fno_w_out[0].astype(BF16),
           w_in[:, :ssd_width + conv_dim].astype(BF16), w_in[:, ssd_width + conv_dim:].astype(BF16),
           ssd_conv_w[0], ssd_conv_b[0], ssd_dt_bias[0], ssd_a_log[0],
           jnp.repeat(ssd_d[0], SSD_HEAD_DIM).reshape(1, ssd_width),
           ssd_norm_w[0], ssd_w_out[0].astype(BF16))

    y_prompt, fin = _trunk(x_prompt.reshape(batch * seq, d), None, batch, seq,
                           False, None, mod5, wts, want_final=True)
    pos = jnp.asarray(_grid_pos_embed(dec_seq, d), dtype=F32)
    y_sample, _ = _trunk(x_sample.reshape(dec_batch * dec_seq, d), pos, dec_batch, dec_seq,
                         True, state_ssd_ctx[:, 0], mod5, wts, want_final=False)
    return (y_prompt.reshape(batch, seq, d), y_sample.reshape(dec_batch, dec_seq, d),
            fin[:, None])
```

```python
import functools
import math

import numpy as np
import jax
import jax.numpy as jnp
from jax import lax
from jax.experimental import pallas as pl
from jax.experimental.pallas import tpu as pltpu

F32 = jnp.float32
BF16 = jnp.bfloat16

DEPTH = 2
GRID_W = 64
FNO_GROUP_DIM = 256
SSD_HEAD_DIM = 64
SSD_GROUPS = 8
SSD_D_STATE = 128
SSD_CONV_W = 5
CHUNK = 128
DEEPNORM_ALPHA = (2 * DEPTH) ** 0.25
LN_EPS = 1e-5
COND_ROWS = 16
CTX_ROW = 8
LANES = 128
MIB = 1024 * 1024


def _params(semantics, vmem_mib):
    return pltpu.CompilerParams(dimension_semantics=semantics, vmem_limit_bytes=vmem_mib * MIB)


def _silu(x):
    return x * jax.nn.sigmoid(x)


def _dot(a, b):
    return jnp.dot(a, b, preferred_element_type=F32)


def _mod_kernel(c_ref, w_ref, b_ref, o_ref):
    s = _silu(c_ref[...]).astype(BF16)
    o_ref[...] = _dot(s, w_ref[...].astype(BF16)) + b_ref[...]


def _modulation(cond, w_ada, b_ada):
    depth, d, n = w_ada.shape
    tn = 512
    return pl.pallas_call(
        _mod_kernel,
        grid=(depth, n // tn),
        in_specs=[
            pl.BlockSpec((COND_ROWS, d), lambda i, j: (0, 0)),
            pl.BlockSpec((None, d, tn), lambda i, j: (i, 0, j)),
            pl.BlockSpec((None, 1, tn), lambda i, j: (i, 0, j)),
        ],
        out_specs=pl.BlockSpec((None, COND_ROWS, tn), lambda i, j: (i, 0, j)),
        out_shape=jax.ShapeDtypeStruct((depth, COND_ROWS, n), F32),
        compiler_params=_params(("parallel", "parallel"), 32),
        name="adaln_mod",
    )(cond, w_ada, b_ada.reshape(depth, 1, n))


def _ln_modulate(x, shift, scale):
    mu = jnp.mean(x, axis=-1, keepdims=True)
    xc = x - mu
    var = jnp.mean(xc * xc, axis=-1, keepdims=True)
    return xc * lax.rsqrt(var + LN_EPS) * (1.0 + scale) + shift


def _ln_mm_kernel(*refs, has_pos, has_dt):
    refs = list(refs)
    x_ref = refs.pop(0)
    pos_ref = refs.pop(0) if has_pos else None
    shift_ref, scale_ref, w_ref = refs.pop(0), refs.pop(0), refs.pop(0)
    wdt_ref = refs.pop(0) if has_dt else None
    o_ref = refs.pop(0)
    dt_ref = refs.pop(0) if has_dt else None
    h_scr = refs.pop(0)

    @pl.when(pl.program_id(1) == 0)
    def _():
        x = x_ref[...]
        if has_pos:
            x = x + pos_ref[...]
        h = _ln_modulate(x, shift_ref[...], scale_ref[...]).astype(BF16)
        h_scr[...] = h
        if has_dt:
            dt_ref[...] = _dot(h, wdt_ref[...])

    o_ref[...] = _dot(h_scr[...], w_ref[...]).astype(o_ref.dtype)


def _ln_in_proj(x, pos, mod5, layer, cond_row, w, w_dt, *, tm, tn):
    t, d = x.shape
    n = w.shape[1]
    has_pos, has_dt = pos is not None, w_dt is not None
    in_specs = [pl.BlockSpec((tm, d), lambda i, j: (i, 0))]
    args = [x]
    if has_pos:
        pos_blocks = pos.shape[0] // tm
        in_specs.append(pl.BlockSpec((tm, d), lambda i, j: (i % pos_blocks, 0)))
        args.append(pos)
    for part in (0, 1):
        in_specs.append(pl.BlockSpec((None, None, None, 1, d),
                                     lambda i, j, part=part: (layer, cond_row(i), part, 0, 0)))
        args.append(mod5)
    in_specs.append(pl.BlockSpec((d, tn), lambda i, j: (0, j)))
    args.append(w)
    out_specs = [pl.BlockSpec((tm, tn), lambda i, j: (i, j))]
    out_shape = [jax.ShapeDtypeStruct((t, n), BF16)]
    if has_dt:
        in_specs.append(pl.BlockSpec((d, LANES), lambda i, j: (0, 0)))
        args.append(w_dt)
        out_specs.append(pl.BlockSpec((tm, LANES), lambda i, j: (i, 0)))
        out_shape.append(jax.ShapeDtypeStruct((t, LANES), F32))
    out = pl.pallas_call(
        functools.partial(_ln_mm_kernel, has_pos=has_pos, has_dt=has_dt),
        grid=(t // tm, n // tn),
        in_specs=in_specs,
        out_specs=out_specs,
        out_shape=out_shape,
        scratch_shapes=[pltpu.VMEM((tm, d), BF16)],
        compiler_params=_params(("parallel", "arbitrary"), 48),
        name="ln_in_proj",
    )(*args)
    return out if has_dt else out[0]


def _dft_matrices(seq, group_dim):
    def cos_sin(n):
        k = np.arange(n, dtype=np.int64)
        ang = 2.0 * np.pi * ((k[:, None] * k[None, :]) % n).astype(np.float64) / n
        return np.cos(ang), np.sin(ang)
    cl, sl = cos_sin(seq)
    cc, sc = cos_sin(group_dim)
    norm = 1.0 / math.sqrt(seq * group_dim)
    chan = np.concatenate([cc, sc], axis=1)
    posm = np.concatenate([cl * norm, -sl * norm], axis=1)
    return jnp.asarray(chan, dtype=F32).astype(BF16), jnp.asarray(posm, dtype=F32).astype(BF16)


def _fno_kernel(u_ref, z_ref, chan_ref, posm_ref, o_ref):
    seq, tw = u_ref.shape
    cg = FNO_GROUP_DIM
    for k in range(tw // cg):
        cols = slice(k * cg, (k + 1) * cg)
        t = _dot(u_ref[:, cols], chan_ref[...])
        y = (_dot(posm_ref[:, :seq], t[:, :cg].astype(BF16))
             + _dot(posm_ref[:, seq:], t[:, cg:].astype(BF16)))
        o_ref[:, cols] = (y * _silu(z_ref[:, cols].astype(F32))).astype(o_ref.dtype)


def _fno_core(uz, batch, seq, width, *, tw):
    chan, posm = _dft_matrices(seq, FNO_GROUP_DIM)
    nw = width // tw
    return pl.pallas_call(
        _fno_kernel,
        grid=(batch, nw),
        in_specs=[
            pl.BlockSpec((seq, tw), lambda b, j: (b, j)),
            pl.BlockSpec((seq, tw), lambda b, j: (b, nw + j)),
            pl.BlockSpec(chan.shape, lambda b, j: (0, 0)),
            pl.BlockSpec(posm.shape, lambda b, j: (0, 0)),
        ],
        out_specs=pl.BlockSpec((seq, tw), lambda b, j: (b, j)),
        out_shape=jax.ShapeDtypeStruct((batch * seq, width), BF16),
        compiler_params=_params(("parallel", "parallel"), 48),
        name="fno_core",
    )(uz, uz, chan, posm)


def _out_ln_kernel(*refs, has_pos, gated):
    refs = list(refs)
    a_ref = refs.pop(0)
    z_ref, nw_ref = (refs.pop(0), refs.pop(0)) if gated else (None, None)
    w_ref, x_ref = refs.pop(0), refs.pop(0)
    pos_ref = refs.pop(0) if has_pos else None
    gate_ref, g_ref, b_ref, o_ref = refs

    if gated:
        y = a_ref[...] * _silu(z_ref[...].astype(F32))
        ms = jnp.mean(y * y, axis=-1, keepdims=True)
        a = (y * lax.rsqrt(ms + LN_EPS) * nw_ref[...]).astype(BF16)
    else:
        a = a_ref[...]
    x = x_ref[...]
    if has_pos:
        x = x + pos_ref[...]
    r = DEEPNORM_ALPHA * x + gate_ref[...] * _dot(a, w_ref[...])
    mu = jnp.mean(r, axis=-1, keepdims=True)
    rc = r - mu
    var = jnp.mean(rc * rc, axis=-1, keepdims=True)
    o_ref[...] = rc * lax.rsqrt(var + LN_EPS) * g_ref[...] + b_ref[...]


def _out_proj_ln(a, z, norm_w, w, x, pos, mod5, layer, cond_row, ln_g, ln_b, *, tm):
    t, kdim = a.shape
    d = w.shape[1]
    has_pos, gated = pos is not None, z is not None
    row_k = pl.BlockSpec((tm, kdim), lambda i: (i, 0))
    row_d = pl.BlockSpec((tm, d), lambda i: (i, 0))
    in_specs, args = [row_k], [a]
    if gated:
        in_specs += [row_k, pl.BlockSpec((1, kdim), lambda i: (0, 0))]
        args += [z, norm_w.reshape(1, kdim)]
    in_specs += [pl.BlockSpec((kdim, d), lambda i: (0, 0), pipeline_mode=pl.Buffered(1)), row_d]
    args += [w, x]
    if has_pos:
        pos_blocks = pos.shape[0] // tm
        in_specs.append(pl.BlockSpec((tm, d), lambda i: (i % pos_blocks, 0)))
        args.append(pos)
    in_specs.append(pl.BlockSpec((None, None, None, 1, d), lambda i: (layer, cond_row(i), 2, 0, 0)))
    args.append(mod5)
    for v in (ln_g, ln_b):
        in_specs.append(pl.BlockSpec((None, 1, d), lambda i: (layer, 0, 0)))
        args.append(v.reshape(v.shape[0], 1, d))
    return pl.pallas_call(
        functools.partial(_out_ln_kernel, has_pos=has_pos, gated=gated),
        grid=(t // tm,),
        in_specs=in_specs,
        out_specs=row_d,
        out_shape=jax.ShapeDtypeStruct((t, d), F32),
        compiler_params=_params(("parallel",), 56),
        name="out_proj_ln",
    )(*args)


def _conv_kernel(x_ref, w_ref, b_ref, o_ref, pad_scr):
    seq, tc = x_ref.shape
    halo = 8
    pad_scr[0:halo, :] = jnp.zeros((halo, tc), F32)
    pad_scr[halo + seq:2 * halo + seq, :] = jnp.zeros((halo, tc), F32)
    pad_scr[halo:halo + seq, :] = x_ref[...].astype(F32)
    padded = pad_scr[...]
    acc = jnp.broadcast_to(b_ref[...], (seq, tc))
    for k in range(SSD_CONV_W):
        shift = (SSD_CONV_W // 2 - k) % (seq + 2 * halo)
        tap = padded if shift == 0 else pltpu.roll(padded, shift, axis=0)
        acc = acc + tap[halo:halo + seq, :] * w_ref[k:k + 1, :]
    o_ref[...] = _silu(acc).astype(o_ref.dtype)


def _conv_silu(zx, conv_w, conv_b, batch, seq, col0, *, tc):
    conv_dim = conv_w.shape[1]
    c0 = col0 // tc
    w8 = jnp.zeros((8, conv_dim), F32).at[:SSD_CONV_W].set(conv_w)
    return pl.pallas_call(
        _conv_kernel,
        grid=(batch, conv_dim // tc),
        in_specs=[pl.BlockSpec((seq, tc), lambda b, j: (b, c0 + j)),
                  pl.BlockSpec((8, tc), lambda b, j: (0, j)),
                  pl.BlockSpec((1, tc), lambda b, j: (0, j))],
        out_specs=pl.BlockSpec((seq, tc), lambda b, j: (b, j)),
        out_shape=jax.ShapeDtypeStruct((batch * seq, conv_dim), BF16),
        scratch_shapes=[pltpu.VMEM((seq + 16, tc), F32)],
        compiler_params=_params(("parallel", "parallel"), 48),
        name="ssd_conv_silu",
    )(zx, w8, conv_b.reshape(1, conv_dim))


def _split3(x):
    hi = x.astype(BF16)
    r = x - hi.astype(F32)
    mid = r.astype(BF16)
    lo = (r - mid.astype(F32)).astype(BF16)
    return hi, mid, lo


HEADS_PER_GROUP = 8
PAIRS = HEADS_PER_GROUP // 2
GROUP_W = HEADS_PER_GROUP * SSD_HEAD_DIM
N_DH = 2 * HEADS_PER_GROUP
SPLIT = 3
E_COL0 = N_DH * SPLIT
LOG2E = 1.4426950408889634


def _group_select_matrix():
    sel = np.zeros((2 * SPLIT * LANES, SSD_GROUPS * LANES), np.float32)
    half = LANES // 2
    for g in range(SSD_GROUPS):
        for j in range(N_DH):
            d, h = divmod(j, HEADS_PER_GROUP)
            src = d * half + g * HEADS_PER_GROUP + h
            for p in range(SPLIT):
                sel[p * LANES + src, g * LANES + SPLIT * j + p] = 1.0
                sel[(SPLIT + p) * LANES + src, g * LANES + E_COL0 + SPLIT * j + p] = 1.0
    return jnp.asarray(sel, dtype=BF16)


def _lane_bcast_matrices():
    e = np.zeros((2, LANES, (HEADS_PER_GROUP + PAIRS) * LANES), np.float32)
    for d in range(2):
        for h in range(HEADS_PER_GROUP):
            j = d * HEADS_PER_GROUP + h
            e[d, SPLIT * j:SPLIT * (j + 1), h * LANES:(h + 1) * LANES] = 1.0
            lane0 = (HEADS_PER_GROUP + h // 2) * LANES + (h % 2) * SSD_HEAD_DIM
            e[d, E_COL0 + SPLIT * j:E_COL0 + SPLIT * (j + 1), lane0:lane0 + SSD_HEAD_DIM] = 1.0
    return jnp.asarray(e, dtype=BF16)


def _dt_kernel(raw_ref, bias_ref, alog_ref, sel_ref, col_ref, row_ref):
    n_ch = row_ref.shape[1]
    half = LANES // 2
    row = lax.broadcasted_iota(jnp.int32, (CHUNK, CHUNK), 0)
    col = lax.broadcasted_iota(jnp.int32, (CHUNK, CHUNK), 1)
    tri_f = jnp.where(row >= col, 1.0, 0.0).astype(BF16)
    tri_b = jnp.where(row <= col, 1.0, 0.0).astype(BF16)
    is_fwd = col < half
    a2 = -jnp.exp(alog_ref[...]) * LOG2E
    for k in range(n_ch):
        rows = slice(k * CHUNK, (k + 1) * CHUNK)
        x = raw_ref[rows, :] + bias_ref[...]
        dt = jnp.maximum(x, 0.0) + jnp.log1p(jnp.exp(-jnp.abs(x)))
        parts = _split3(dt * a2)
        cs = jnp.where(is_fwd, sum(_dot(tri_f, p) for p in parts), sum(_dot(tri_b, p) for p in parts))
        tot = jnp.where(is_fwd[:1], cs[CHUNK - 1:CHUNK, :], cs[0:1, :])
        w = dt * jnp.exp2(tot - cs)
        pieces = jnp.concatenate(list(_split3(cs)) + list(_split3(jnp.exp2(cs))), axis=1)
        grouped = _dot(pieces, sel_ref[...]).astype(BF16)
        transposed = (cs.T, dt.T, w.T)
        for g in range(SSD_GROUPS):
            col_ref[g, rows, :] = grouped[:, g * LANES:(g + 1) * LANES]
            for n, v in enumerate(transposed):
                for d in (0, 1):
                    src = d * half + g * HEADS_PER_GROUP
                    dst = n * N_DH + d * HEADS_PER_GROUP
                    row_ref[g, k, dst:dst + HEADS_PER_GROUP, :] = v[src:src + HEADS_PER_GROUP, :]


def _dt_prep(dt_raw, dt_bias, a_log, *, n_ch):
    t = dt_raw.shape[0]
    tm = n_ch * CHUNK
    sel = _group_select_matrix()
    vec_spec = pl.BlockSpec((1, LANES), lambda i: (0, 0))
    return pl.pallas_call(
        _dt_kernel,
        grid=(t // tm,),
        in_specs=[pl.BlockSpec((tm, LANES), lambda i: (i, 0)), vec_spec, vec_spec,
                  pl.BlockSpec(sel.shape, lambda i: (0, 0))],
        out_specs=[pl.BlockSpec((SSD_GROUPS, tm, LANES), lambda i: (0, i, 0)),
                   pl.BlockSpec((SSD_GROUPS, n_ch, 3 * N_DH, CHUNK), lambda i: (0, i, 0, 0))],
        out_shape=[jax.ShapeDtypeStruct((SSD_GROUPS, t, LANES), BF16),
                   jax.ShapeDtypeStruct((SSD_GROUPS, t // CHUNK, 3 * N_DH, CHUNK), F32)],
        compiler_params=_params(("parallel",), 32),
        name="ssd_dt_prep",
    )(dt_raw, dt_bias.reshape(1, LANES), a_log.reshape(1, LANES), sel)


def _ssd_kernel(*refs, nc, has_init, want_final):
    refs = list(refs)
    xs_ref, b_ref, c_ref, col_ref, row_ref, bcast_ref, dsk_ref = refs[:7]
    del refs[:7]
    init_ref = refs.pop(0) if has_init else None
    y_ref = refs.pop(0)
    fin_ref = refs.pop(0) if want_final else None
    st_scr = refs.pop(0)

    p = SSD_HEAD_DIM
    row = lax.broadcasted_iota(jnp.int32, (CHUNK, CHUNK), 0)
    col = lax.broadcasted_iota(jnp.int32, (CHUNK, CHUNK), 1)
    lo_half = col < p

    for d in (0, 1):
        for q in range(PAIRS):
            lanes = slice(q * LANES, (q + 1) * LANES)
            if has_init:
                blk = init_ref[d, 2 * q:2 * q + 2].reshape(2 * p, SSD_D_STATE)
                st_scr[d, :, lanes] = blk.T
            else:
                st_scr[d, :, lanes] = jnp.zeros((SSD_D_STATE, LANES), F32)
    y_ref[...] = dsk_ref[...] * xs_ref[...].astype(F32)

    def chunk_body(i, carry):
        for d in (0, 1):
            causal = (row >= col) if d == 0 else (row <= col)
            c = i if d == 0 else nc - 1 - i
            r0 = pl.multiple_of(c * CHUNK, CHUNK)
            rows = pl.ds(r0, CHUNK)
            bc = b_ref[rows, :]
            cc = c_ref[rows, :]
            g = lax.dot_general(cc, bc, (((1,), (1,)), ((), ())), preferred_element_type=F32)
            gm = jnp.where(causal, g, 0.0)
            bt = bc.astype(F32).T
            cols = col_ref[rows, :]
            edge = CHUNK - 1 if d == 0 else 0
            y_off = _dot(cc, st_scr[d].astype(BF16))
            spread = _dot(cols, bcast_ref[d])
            xbd_q, m_q, bt_q = [], [], []
            for q in range(PAIRS):
                x = xs_ref[rows, q * LANES:(q + 1) * LANES]
                zero = jnp.zeros_like(x)
                xbd_q.append(jnp.concatenate([jnp.where(lo_half, x, zero), jnp.where(lo_half, zero, x)], axis=0))
                m_h, bt_h = [], []
                for h in (2 * q, 2 * q + 1):
                    j = d * HEADS_PER_GROUP + h
                    cs_l = spread[:, h * LANES:(h + 1) * LANES]
                    cs_s = row_ref[c, pl.ds(j, 1), :]
                    dt_s = row_ref[c, pl.ds(N_DH + j, 1), :]
                    w_s = row_ref[c, pl.ds(2 * N_DH + j, 1), :]
                    lmat = jnp.exp2(jnp.minimum(cs_l - cs_s, 0.0))
                    m_h.append((gm * lmat * dt_s).astype(BF16))
                    bt_h.append((bt * w_s).astype(BF16))
                m_q.append(jnp.concatenate(m_h, axis=1))
                bt_q.append(jnp.concatenate(bt_h, axis=1))
            y_diag = [_dot(m_q[q], xbd_q[q]) for q in range(PAIRS)]
            s_new = [_dot(bt_q[q], xbd_q[q]) for q in range(PAIRS)]
            for q in range(PAIRS):
                lanes = slice(q * LANES, (q + 1) * LANES)
                e_q = spread[:, (HEADS_PER_GROUP + q) * LANES:(HEADS_PER_GROUP + q + 1) * LANES]
                y_ref[rows, lanes] += y_diag[q] + y_off[:, lanes] * e_q
                st_scr[d, :, lanes] = st_scr[d, :, lanes] * e_q[edge:edge + 1, :] + s_new[q]
        return carry

    lax.fori_loop(0, nc, chunk_body, 0)

    if want_final:
        for d in (0, 1):
            for q in range(PAIRS):
                lanes = slice(q * LANES, (q + 1) * LANES)
                fin_ref[d, 2 * q:2 * q + 2] = st_scr[d, :, lanes].T.reshape(2, p, SSD_D_STATE)


def _ssd_scan(xbc, cols_g, rows_g, d_skip, init, batch, seq, *, want_final):
    nc = seq // CHUNK
    width = SSD_GROUPS * GROUP_W
    heads = SSD_GROUPS * HEADS_PER_GROUP
    has_init = init is not None
    bcast = _lane_bcast_matrices()
    in_specs = [
        pl.BlockSpec((seq, GROUP_W), lambda b, g: (b, g)),
        pl.BlockSpec((seq, SSD_D_STATE), lambda b, g: (b, width // SSD_D_STATE + g)),
        pl.BlockSpec((seq, SSD_D_STATE), lambda b, g: (b, width // SSD_D_STATE + SSD_GROUPS + g)),
        pl.BlockSpec((None, seq, LANES), lambda b, g: (g, b, 0)),
        pl.BlockSpec((None, nc, 3 * N_DH, CHUNK), lambda b, g: (g, b, 0, 0)),
        pl.BlockSpec(bcast.shape, lambda b, g: (0, 0, 0)),
        pl.BlockSpec((1, GROUP_W), lambda b, g: (0, g)),
    ]
    args = [xbc, xbc, xbc, cols_g, rows_g, bcast, d_skip]
    state_spec = pl.BlockSpec((None, 2, HEADS_PER_GROUP, SSD_HEAD_DIM, SSD_D_STATE),
                              lambda b, g: (b, 0, g, 0, 0))
    if has_init:
        in_specs.append(state_spec)
        args.append(init)
    out_specs = [pl.BlockSpec((seq, GROUP_W), lambda b, g: (b, g))]
    out_shape = [jax.ShapeDtypeStruct((batch * seq, width), F32)]
    if want_final:
        out_specs.append(state_spec)
        out_shape.append(jax.ShapeDtypeStruct((batch, 2, heads, SSD_HEAD_DIM, SSD_D_STATE), F32))
    out = pl.pallas_call(
        functools.partial(_ssd_kernel, nc=nc, has_init=has_init, want_final=want_final),
        grid=(batch, SSD_GROUPS),
        in_specs=in_specs,
        out_specs=out_specs,
        out_shape=out_shape,
        scratch_shapes=[pltpu.VMEM((2, SSD_D_STATE, GROUP_W), F32)],
        compiler_params=_params(("parallel", "parallel"), 48),
        name="ssd_scan",
    )(*args)
    return out if want_final else (out[0], None)


def _trunk(x, pos, batch, seq, per_seq_cond, init, mod5, wts, *, want_final):
    (ln_g, ln_b, fno_w_in, fno_w_out, ssd_w_zx, ssd_w_dt, conv_w, conv_b, dt_bias, a_log,
     d_skip, norm_w, ssd_w_out) = wts
    fno_width = fno_w_out.shape[0]
    ssd_width = ssd_w_out.shape[0]
    tm_in, tm_out = 512, 256

    def cond_row(tm):
        if not per_seq_cond:
            return lambda i: CTX_ROW
        assert seq % tm == 0
        return lambda i: (i * tm) // seq

    uz = _ln_in_proj(x, pos, mod5, 0, cond_row(tm_in), fno_w_in, None, tm=tm_in, tn=2048)
    yg = _fno_core(uz, batch, seq, fno_width, tw=min(fno_width, 4096 * 256 // seq))
    x1 = _out_proj_ln(yg, None, None, fno_w_out, x, pos, mod5, 0, cond_row(tm_out), ln_g, ln_b, tm=tm_out)

    zx, dt_raw = _ln_in_proj(x1, None, mod5, 1, cond_row(tm_in), ssd_w_zx, ssd_w_dt, tm=tm_in, tn=2048)
    xbc = _conv_silu(zx, conv_w, conv_b, batch, seq, ssd_width, tc=1024)
    cols_g, rows_g = _dt_prep(dt_raw, dt_bias, a_log, n_ch=8)
    y, fin = _ssd_scan(xbc, cols_g, rows_g, d_skip, init, batch, seq, want_final=want_final)
    x2 = _out_proj_ln(y, zx, norm_w, ssd_w_out, x1, None, mod5, 1, cond_row(tm_out), ln_g, ln_b, tm=tm_out)
    return x2, fin


def _sincos(pos, dim):
    omega = 1.0 / (10000.0 ** (np.arange(dim // 2, dtype=np.float64) / (dim / 2)))
    ang = pos.astype(np.float64)[:, None] * omega[None, :]
    return np.concatenate([np.sin(ang), np.cos(ang)], axis=-1)


def _grid_pos_embed(n_tokens, dim):
    t = np.arange(n_tokens)
    return np.concatenate([_sincos(t // GRID_W, dim // 2), _sincos(t % GRID_W, dim // 2)], axis=-1)


def kernel(x_prompt, x_sample, state_ssd_ctx, c, c_ctx, w_ada, b_ada, ln_g, ln_b, fno_w_in, fno_w_out,
           ssd_w_in, ssd_conv_w, ssd_conv_b, ssd_dt_bias, ssd_a_log, ssd_d, ssd_norm_w, ssd_w_out):
    batch, seq, d = x_prompt.shape
    dec_batch, dec_seq, _ = x_sample.shape
    ssd_width = ssd_w_out.shape[1]
    conv_dim = ssd_conv_w.shape[2]

    cond = jnp.zeros((COND_ROWS, d), F32).at[:dec_batch].set(c).at[CTX_ROW].set(c_ctx)
    mod = _modulation(cond, w_ada, b_ada)
    mod5 = mod.reshape(DEPTH, COND_ROWS, 3, 1, d)

    w_in = ssd_w_in[0]
    wts = (ln_g, ln_b,
           fno_w_in[0].astype(BF16), fno_w_out[0].astype(BF16),
           w_in[:, :ssd_width + conv_dim].astype(BF16), w_in[:, ssd_width + conv_dim:].astype(BF16),
           ssd_conv_w[0], ssd_conv_b[0], ssd_dt_bias[0], ssd_a_log[0],
           jnp.repeat(ssd_d[0], SSD_HEAD_DIM).reshape(1, ssd_width),
           ssd_norm_w[0], ssd_w_out[0].astype(BF16))

    y_prompt, fin = _trunk(x_prompt.reshape(batch * seq, d), None, batch, seq,
                           False, None, mod5, wts, want_final=True)
    pos = jnp.asarray(_grid_pos_embed(dec_seq, d), dtype=F32)
    y_sample, _ = _trunk(x_sample.reshape(dec_batch * dec_seq, d), pos, dec_batch, dec_seq,
                         True, state_ssd_ctx[:, 0], mod5, wts, want_final=False)
    return (y_prompt.reshape(batch, seq, d), y_sample.reshape(dec_batch, dec_seq, d),
            fin[:, None])
```

```python
import functools
import math

import numpy as np
import jax
import jax.numpy as jnp
from jax import lax
from jax.experimental import pallas as pl
from jax.experimental.pallas import tpu as pltpu

F32 = jnp.float32
BF16 = jnp.bfloat16

DEPTH = 2
GRID_W = 64
FNO_GROUP_DIM = 256
SSD_HEAD_DIM = 64
SSD_GROUPS = 8
SSD_D_STATE = 128
SSD_CONV_W = 5
CHUNK = 128
DEEPNORM_ALPHA = (2 * DEPTH) ** 0.25
LN_EPS = 1e-5
COND_ROWS = 16
CTX_ROW = 8
LANES = 128
MIB = 1024 * 1024


def _params(semantics, vmem_mib):
    return pltpu.CompilerParams(dimension_semantics=semantics, vmem_limit_bytes=vmem_mib * MIB)


def _silu(x):
    return x * jax.nn.sigmoid(x)


def _dot(a, b):
    return jnp.dot(a, b, preferred_element_type=F32)


def _mod_kernel(c_ref, w_ref, b_ref, o_ref):
    s = _silu(c_ref[...]).astype(BF16)
    o_ref[...] = _dot(s, w_ref[...].astype(BF16)) + b_ref[...]


def _modulation(cond, w_ada, b_ada):
    depth, d, n = w_ada.shape
    tn = 512
    return pl.pallas_call(
        _mod_kernel,
        grid=(depth, n // tn),
        in_specs=[
            pl.BlockSpec((COND_ROWS, d), lambda i, j: (0, 0)),
            pl.BlockSpec((None, d, tn), lambda i, j: (i, 0, j)),
            pl.BlockSpec((None, 1, tn), lambda i, j: (i, 0, j)),
        ],
        out_specs=pl.BlockSpec((None, COND_ROWS, tn), lambda i, j: (i, 0, j)),
        out_shape=jax.ShapeDtypeStruct((depth, COND_ROWS, n), F32),
        compiler_params=_params(("parallel", "parallel"), 32),
        name="adaln_mod",
    )(cond, w_ada, b_ada.reshape(depth, 1, n))


LN_ROWS = 256


def _ln_modulate(x, shift, scale):
    mu = jnp.mean(x, axis=-1, keepdims=True)
    xc = x - mu
    var = jnp.mean(xc * xc, axis=-1, keepdims=True)
    return xc * lax.rsqrt(var + LN_EPS) * (1.0 + scale) + shift


def _ln_mm_kernel(*refs, has_pos, has_dt):
    refs = list(refs)
    x_ref = refs.pop(0)
    pos_ref = refs.pop(0) if has_pos else None
    shift_ref, scale_ref, w_ref = refs.pop(0), refs.pop(0), refs.pop(0)
    wdt_ref = refs.pop(0) if has_dt else None
    o_ref = refs.pop(0)
    dt_ref = refs.pop(0) if has_dt else None
    h_scr = refs.pop(0)

    @pl.when(pl.program_id(1) == 0)
    def _():
        piece = min(LN_ROWS, x_ref.shape[0])
        for r in range(0, x_ref.shape[0], piece):
            rows = slice(r, r + piece)
            x = x_ref[rows, :]
            if has_pos:
                x = x + pos_ref[rows, :]
            h = _ln_modulate(x, shift_ref[...], scale_ref[...]).astype(BF16)
            h_scr[rows, :] = h
            if has_dt:
                dt_ref[rows, :] = _dot(h, wdt_ref[...])

    o_ref[...] = _dot(h_scr[...], w_ref[...]).astype(o_ref.dtype)


def _ln_in_proj(x, pos, mod5, layer, cond_row, w, w_dt, *, tm, tn):
    t, d = x.shape
    n = w.shape[1]
    has_pos, has_dt = pos is not None, w_dt is not None
    in_specs = [pl.BlockSpec((tm, d), lambda i, j: (i, 0))]
    args = [x]
    if has_pos:
        pos_blocks = pos.shape[0] // tm
        in_specs.append(pl.BlockSpec((tm, d), lambda i, j: (i % pos_blocks, 0)))
        args.append(pos)
    for part in (0, 1):
        in_specs.append(pl.BlockSpec((None, None, None, 1, d),
                                     lambda i, j, part=part: (layer, cond_row(i), part, 0, 0)))
        args.append(mod5)
    in_specs.append(pl.BlockSpec((d, tn), lambda i, j: (0, j)))
    args.append(w)
    out_specs = [pl.BlockSpec((tm, tn), lambda i, j: (i, j))]
    out_shape = [jax.ShapeDtypeStruct((t, n), BF16)]
    if has_dt:
        in_specs.append(pl.BlockSpec((d, LANES), lambda i, j: (0, 0)))
        args.append(w_dt)
        out_specs.append(pl.BlockSpec((tm, LANES), lambda i, j: (i, 0)))
        out_shape.append(jax.ShapeDtypeStruct((t, LANES), F32))
    out = pl.pallas_call(
        functools.partial(_ln_mm_kernel, has_pos=has_pos, has_dt=has_dt),
        grid=(t // tm, n // tn),
        in_specs=in_specs,
        out_specs=out_specs,
        out_shape=out_shape,
        scratch_shapes=[pltpu.VMEM((tm, d), BF16)],
        compiler_params=_params(("parallel", "arbitrary"), 56),
        name="ln_in_proj",
    )(*args)
    return out if has_dt else out[0]


def _dft_matrices(seq, group_dim):
    def cos_sin(n):
        k = np.arange(n, dtype=np.int64)
        ang = 2.0 * np.pi * ((k[:, None] * k[None, :]) % n).astype(np.float64) / n
        return np.cos(ang), np.sin(ang)
    cl, sl = cos_sin(seq)
    cc, sc = cos_sin(group_dim)
    norm = 1.0 / math.sqrt(seq * group_dim)
    chan = np.concatenate([cc, sc], axis=1)
    posm = np.concatenate([cl * norm, -sl * norm], axis=1)
    return jnp.asarray(chan, dtype=F32).astype(BF16), jnp.asarray(posm, dtype=F32).astype(BF16)


def _fno_kernel(u_ref, z_ref, chan_ref, posm_ref, o_ref):
    seq, tw = u_ref.shape
    cg = FNO_GROUP_DIM
    for k in range(tw // cg):
        cols = slice(k * cg, (k + 1) * cg)
        t = _dot(u_ref[:, cols], chan_ref[...])
        y = (_dot(posm_ref[:, :seq], t[:, :cg].astype(BF16))
             + _dot(posm_ref[:, seq:], t[:, cg:].astype(BF16)))
        o_ref[:, cols] = (y * _silu(z_ref[:, cols].astype(F32))).astype(o_ref.dtype)


def _fno_core(uz, batch, seq, width, *, tw):
    chan, posm = _dft_matrices(seq, FNO_GROUP_DIM)
    nw = width // tw
    return pl.pallas_call(
        _fno_kernel,
        grid=(batch, nw),
        in_specs=[
            pl.BlockSpec((seq, tw), lambda b, j: (b, j)),
            pl.BlockSpec((seq, tw), lambda b, j: (b, nw + j)),
            pl.BlockSpec(chan.shape, lambda b, j: (0, 0)),
            pl.BlockSpec(posm.shape, lambda b, j: (0, 0)),
        ],
        out_specs=pl.BlockSpec((seq, tw), lambda b, j: (b, j)),
        out_shape=jax.ShapeDtypeStruct((batch * seq, width), BF16),
        compiler_params=_params(("parallel", "parallel"), 48),
        name="fno_core",
    )(uz, uz, chan, posm)


def _out_ln_kernel(*refs, has_pos, gated):
    refs = list(refs)
    a_ref = refs.pop(0)
    z_ref, nw_ref = (refs.pop(0), refs.pop(0)) if gated else (None, None)
    w_ref, x_ref = refs.pop(0), refs.pop(0)
    pos_ref = refs.pop(0) if has_pos else None
    gate_ref, g_ref, b_ref, o_ref = refs

    if gated:
        y = a_ref[...].astype(F32) * _silu(z_ref[...].astype(F32))
        ms = jnp.mean(y * y, axis=-1, keepdims=True)
        a = (y * lax.rsqrt(ms + LN_EPS) * nw_ref[...]).astype(BF16)
    else:
        a = a_ref[...]
    x = x_ref[...]
    if has_pos:
        x = x + pos_ref[...]
    r = DEEPNORM_ALPHA * x + gate_ref[...] * _dot(a, w_ref[...])
    mu = jnp.mean(r, axis=-1, keepdims=True)
    rc = r - mu
    var = jnp.mean(rc * rc, axis=-1, keepdims=True)
    o_ref[...] = rc * lax.rsqrt(var + LN_EPS) * g_ref[...] + b_ref[...]


def _out_proj_ln(a, z, norm_w, w, x, pos, mod5, layer, cond_row, ln_g, ln_b, *, tm):
    t, kdim = a.shape
    d = w.shape[1]
    has_pos, gated = pos is not None, z is not None
    row_k = pl.BlockSpec((tm, kdim), lambda i: (i, 0))
    row_d = pl.BlockSpec((tm, d), lambda i: (i, 0))
    in_specs, args = [row_k], [a]
    if gated:
        in_specs += [row_k, pl.BlockSpec((1, kdim), lambda i: (0, 0))]
        args += [z, norm_w.reshape(1, kdim)]
    in_specs += [pl.BlockSpec((kdim, d), lambda i: (0, 0), pipeline_mode=pl.Buffered(1)), row_d]
    args += [w, x]
    if has_pos:
        pos_blocks = pos.shape[0] // tm
        in_specs.append(pl.BlockSpec((tm, d), lambda i: (i % pos_blocks, 0)))
        args.append(pos)
    in_specs.append(pl.BlockSpec((None, None, None, 1, d), lambda i: (layer, cond_row(i), 2, 0, 0)))
    args.append(mod5)
    for v in (ln_g, ln_b):
        in_specs.append(pl.BlockSpec((None, 1, d), lambda i: (layer, 0, 0)))
        args.append(v.reshape(v.shape[0], 1, d))
    return pl.pallas_call(
        functools.partial(_out_ln_kernel, has_pos=has_pos, gated=gated),
        grid=(t // tm,),
        in_specs=in_specs,
        out_specs=row_d,
        out_shape=jax.ShapeDtypeStruct((t, d), F32),
        compiler_params=_params(("parallel",), 56),
        name="out_proj_ln",
    )(*args)


CONV_OFFSETS = (-2, -1, 1, 2)
CONV_PAD = 16
CONV_STRIP = 256


def _shift_matrix():
    s = np.zeros((len(CONV_OFFSETS) * CHUNK, CHUNK + 2 * CONV_PAD), np.float32)
    for t, off in enumerate(CONV_OFFSETS):
        s[t * CHUNK + np.arange(CHUNK), CONV_PAD + np.arange(CHUNK) + off] = 1.0
    return jnp.asarray(s, dtype=BF16)


def _conv_kernel(x_ref, s_ref, w_ref, b_ref, o_ref, pad_scr):
    seq, tc = x_ref.shape
    pad_scr[0:CONV_PAD, :] = jnp.zeros((CONV_PAD, tc), BF16)
    pad_scr[CONV_PAD + seq:2 * CONV_PAD + seq, :] = jnp.zeros((CONV_PAD, tc), BF16)
    pad_scr[CONV_PAD:CONV_PAD + seq, :] = x_ref[...]
    centre = SSD_CONV_W // 2
    for i in range(seq // CHUNK):
        for c in range(0, tc, CONV_STRIP):
            lanes = slice(c, c + CONV_STRIP)
            window = pad_scr[i * CHUNK:(i + 1) * CHUNK + 2 * CONV_PAD, lanes]
            taps = _dot(s_ref[...], window)
            acc = (b_ref[:, lanes]
                   + window[CONV_PAD:CONV_PAD + CHUNK, :].astype(F32) * w_ref[centre:centre + 1, lanes])
            for t, off in enumerate(CONV_OFFSETS):
                acc = acc + taps[t * CHUNK:(t + 1) * CHUNK, :] * w_ref[centre + off:centre + off + 1, lanes]
            o_ref[i * CHUNK:(i + 1) * CHUNK, lanes] = _silu(acc).astype(o_ref.dtype)


def _conv_silu(zx, conv_w, conv_b, batch, seq, col0, *, tc):
    conv_dim = conv_w.shape[1]
    c0 = col0 // tc
    w8 = jnp.zeros((8, conv_dim), F32).at[:SSD_CONV_W].set(conv_w)
    shift = _shift_matrix()
    return pl.pallas_call(
        _conv_kernel,
        grid=(batch, conv_dim // tc),
        in_specs=[pl.BlockSpec((seq, tc), lambda b, j: (b, c0 + j)),
                  pl.BlockSpec(shift.shape, lambda b, j: (0, 0)),
                  pl.BlockSpec((8, tc), lambda b, j: (0, j)),
                  pl.BlockSpec((1, tc), lambda b, j: (0, j))],
        out_specs=pl.BlockSpec((seq, tc), lambda b, j: (b, j)),
        out_shape=jax.ShapeDtypeStruct((batch * seq, conv_dim), BF16),
        scratch_shapes=[pltpu.VMEM((seq + 2 * CONV_PAD, tc), BF16)],
        compiler_params=_params(("parallel", "parallel"), 48),
        name="ssd_conv_silu",
    )(zx, shift, w8, conv_b.reshape(1, conv_dim))


def _split3(x):
    hi = x.astype(BF16)
    r = x - hi.astype(F32)
    mid = r.astype(BF16)
    lo = (r - mid.astype(F32)).astype(BF16)
    return hi, mid, lo


HEADS_PER_GROUP = 8
PAIRS = HEADS_PER_GROUP // 2
GROUP_W = HEADS_PER_GROUP * SSD_HEAD_DIM
N_DH = 2 * HEADS_PER_GROUP
SPLIT = 3
E_COL0 = N_DH * SPLIT
LOG2E = 1.4426950408889634


def _group_select_matrix():
    sel = np.zeros((2 * SPLIT * LANES, SSD_GROUPS * LANES), np.float32)
    half = LANES // 2
    for g in range(SSD_GROUPS):
        for j in range(N_DH):
            d, h = divmod(j, HEADS_PER_GROUP)
            src = d * half + g * HEADS_PER_GROUP + h
            for p in range(SPLIT):
                sel[p * LANES + src, g * LANES + SPLIT * j + p] = 1.0
                sel[(SPLIT + p) * LANES + src, g * LANES + E_COL0 + SPLIT * j + p] = 1.0
    return jnp.asarray(sel, dtype=BF16)


def _lane_bcast_matrices():
    e = np.zeros((2, LANES, (HEADS_PER_GROUP + PAIRS) * LANES), np.float32)
    for d in range(2):
        for h in range(HEADS_PER_GROUP):
            j = d * HEADS_PER_GROUP + h
            e[d, SPLIT * j:SPLIT * (j + 1), h * LANES:(h + 1) * LANES] = 1.0
            lane0 = (HEADS_PER_GROUP + h // 2) * LANES + (h % 2) * SSD_HEAD_DIM
            e[d, E_COL0 + SPLIT * j:E_COL0 + SPLIT * (j + 1), lane0:lane0 + SSD_HEAD_DIM] = 1.0
    return jnp.asarray(e, dtype=BF16)


def _dt_kernel(raw_ref, bias_ref, alog_ref, sel_ref, col_ref, row_ref):
    n_ch = row_ref.shape[1]
    half = LANES // 2
    row = lax.broadcasted_iota(jnp.int32, (CHUNK, CHUNK), 0)
    col = lax.broadcasted_iota(jnp.int32, (CHUNK, CHUNK), 1)
    tri_f = jnp.where(row >= col, 1.0, 0.0).astype(BF16)
    tri_b = jnp.where(row <= col, 1.0, 0.0).astype(BF16)
    is_fwd = col < half
    a2 = -jnp.exp(alog_ref[...]) * LOG2E
    for k in range(n_ch):
        rows = slice(k * CHUNK, (k + 1) * CHUNK)
        x = raw_ref[rows, :] + bias_ref[...]
        dt = jnp.maximum(x, 0.0) + jnp.log1p(jnp.exp(-jnp.abs(x)))
        parts = _split3(dt * a2)
        cs = jnp.where(is_fwd, sum(_dot(tri_f, p) for p in parts), sum(_dot(tri_b, p) for p in parts))
        tot = jnp.where(is_fwd[:1], cs[CHUNK - 1:CHUNK, :], cs[0:1, :])
        w = dt * jnp.exp2(tot - cs)
        pieces = jnp.concatenate(list(_split3(cs)) + list(_split3(jnp.exp2(cs))), axis=1)
        grouped = _dot(pieces, sel_ref[...]).astype(BF16)
        transposed = (cs.T, dt.T, w.T)
        for g in range(SSD_GROUPS):
            col_ref[g, rows, :] = grouped[:, g * LANES:(g + 1) * LANES]
            for n, v in enumerate(transposed):
                for d in (0, 1):
                    src = d * half + g * HEADS_PER_GROUP
                    dst = n * N_DH + d * HEADS_PER_GROUP
                    row_ref[g, k, dst:dst + HEADS_PER_GROUP, :] = v[src:src + HEADS_PER_GROUP, :]


def _dt_prep(dt_raw, dt_bias, a_log, *, n_ch):
    t = dt_raw.shape[0]
    tm = n_ch * CHUNK
    sel = _group_select_matrix()
    vec_spec = pl.BlockSpec((1, LANES), lambda i: (0, 0))
    return pl.pallas_call(
        _dt_kernel,
        grid=(t // tm,),
        in_specs=[pl.BlockSpec((tm, LANES), lambda i: (i, 0)), vec_spec, vec_spec,
                  pl.BlockSpec(sel.shape, lambda i: (0, 0))],
        out_specs=[pl.BlockSpec((SSD_GROUPS, tm, LANES), lambda i: (0, i, 0)),
                   pl.BlockSpec((SSD_GROUPS, n_ch, 3 * N_DH, CHUNK), lambda i: (0, i, 0, 0))],
        out_shape=[jax.ShapeDtypeStruct((SSD_GROUPS, t, LANES), BF16),
                   jax.ShapeDtypeStruct((SSD_GROUPS, t // CHUNK, 3 * N_DH, CHUNK), F32)],
        compiler_params=_params(("parallel",), 32),
        name="ssd_dt_prep",
    )(dt_raw, dt_bias.reshape(1, LANES), a_log.reshape(1, LANES), sel)


def _ssd_kernel(*refs, nc, has_init, want_final):
    refs = list(refs)
    xs_ref, b_ref, c_ref, col_ref, row_ref, bcast_ref, dsk_ref = refs[:7]
    del refs[:7]
    init_ref = refs.pop(0) if has_init else None
    y_ref = refs.pop(0)
    fin_ref = refs.pop(0) if want_final else None
    st_scr, y_scr = refs

    p = SSD_HEAD_DIM
    row = lax.broadcasted_iota(jnp.int32, (CHUNK, CHUNK), 0)
    col = lax.broadcasted_iota(jnp.int32, (CHUNK, CHUNK), 1)
    lo_half = col < p

    for d in (0, 1):
        for q in range(PAIRS):
            lanes = slice(q * LANES, (q + 1) * LANES)
            if has_init:
                blk = init_ref[d, 2 * q:2 * q + 2].reshape(2 * p, SSD_D_STATE)
                st_scr[d, :, lanes] = blk.T
            else:
                st_scr[d, :, lanes] = jnp.zeros((SSD_D_STATE, LANES), F32)
    y_scr[...] = dsk_ref[...] * xs_ref[...].astype(F32)

    def chunk_body(i, carry):
        for d in (0, 1):
            causal = (row >= col) if d == 0 else (row <= col)
            c = i if d == 0 else nc - 1 - i
            r0 = pl.multiple_of(c * CHUNK, CHUNK)
            rows = pl.ds(r0, CHUNK)
            bc = b_ref[rows, :]
            cc = c_ref[rows, :]
            g = lax.dot_general(cc, bc, (((1,), (1,)), ((), ())), preferred_element_type=F32)
            gm = jnp.where(causal, g, 0.0)
            bt = bc.astype(F32).T
            cols = col_ref[rows, :]
            edge = CHUNK - 1 if d == 0 else 0
            y_off = _dot(cc, st_scr[d].astype(BF16))
            spread = _dot(cols, bcast_ref[d])
            xbd_q, m_q, bt_q = [], [], []
            for q in range(PAIRS):
                x = xs_ref[rows, q * LANES:(q + 1) * LANES]
                zero = jnp.zeros_like(x)
                xbd_q.append(jnp.concatenate([jnp.where(lo_half, x, zero), jnp.where(lo_half, zero, x)], axis=0))
                m_h, bt_h = [], []
                for h in (2 * q, 2 * q + 1):
                    j = d * HEADS_PER_GROUP + h
                    cs_l = spread[:, h * LANES:(h + 1) * LANES]
                    cs_s = row_ref[c, pl.ds(j, 1), :]
                    dt_s = row_ref[c, pl.ds(N_DH + j, 1), :]
                    w_s = row_ref[c, pl.ds(2 * N_DH + j, 1), :]
                    lmat = jnp.exp2(jnp.minimum(cs_l - cs_s, 0.0))
                    m_h.append((gm * lmat * dt_s).astype(BF16))
                    bt_h.append((bt * w_s).astype(BF16))
                m_q.append(jnp.concatenate(m_h, axis=1))
                bt_q.append(jnp.concatenate(bt_h, axis=1))
            y_diag = [_dot(m_q[q], xbd_q[q]) for q in range(PAIRS)]
            s_new = [_dot(bt_q[q], xbd_q[q]) for q in range(PAIRS)]
            for q in range(PAIRS):
                lanes = slice(q * LANES, (q + 1) * LANES)
                e_q = spread[:, (HEADS_PER_GROUP + q) * LANES:(HEADS_PER_GROUP + q + 1) * LANES]
                y_scr[rows, lanes] += y_diag[q] + y_off[:, lanes] * e_q
                st_scr[d, :, lanes] = st_scr[d, :, lanes] * e_q[edge:edge + 1, :] + s_new[q]
        return carry

    lax.fori_loop(0, nc, chunk_body, 0)
    y_ref[...] = y_scr[...].astype(y_ref.dtype)

    if want_final:
        for d in (0, 1):
            for q in range(PAIRS):
                lanes = slice(q * LANES, (q + 1) * LANES)
                fin_ref[d, 2 * q:2 * q + 2] = st_scr[d, :, lanes].T.reshape(2, p, SSD_D_STATE)


def _ssd_scan(xbc, cols_g, rows_g, d_skip, init, batch, seq, *, want_final):
    nc = seq // CHUNK
    width = SSD_GROUPS * GROUP_W
    heads = SSD_GROUPS * HEADS_PER_GROUP
    has_init = init is not None
    bcast = _lane_bcast_matrices()
    in_specs = [
        pl.BlockSpec((seq, GROUP_W), lambda b, g: (b, g)),
        pl.BlockSpec((seq, SSD_D_STATE), lambda b, g: (b, width // SSD_D_STATE + g)),
        pl.BlockSpec((seq, SSD_D_STATE), lambda b, g: (b, width // SSD_D_STATE + SSD_GROUPS + g)),
        pl.BlockSpec((None, seq, LANES), lambda b, g: (g, b, 0)),
        pl.BlockSpec((None, nc, 3 * N_DH, CHUNK), lambda b, g: (g, b, 0, 0)),
        pl.BlockSpec(bcast.shape, lambda b, g: (0, 0, 0)),
        pl.BlockSpec((1, GROUP_W), lambda b, g: (0, g)),
    ]
    args = [xbc, xbc, xbc, cols_g, rows_g, bcast, d_skip]
    state_spec = pl.BlockSpec((None, 2, HEADS_PER_GROUP, SSD_HEAD_DIM, SSD_D_STATE),
                              lambda b, g: (b, 0, g, 0, 0))
    if has_init:
        in_specs.append(state_spec)
        args.append(init)
    out_specs = [pl.BlockSpec((seq, GROUP_W), lambda b, g: (b, g))]
    out_shape = [jax.ShapeDtypeStruct((batch * seq, width), BF16)]
    if want_final:
        out_specs.append(state_spec)
        out_shape.append(jax.ShapeDtypeStruct((batch, 2, heads, SSD_HEAD_DIM, SSD_D_STATE), F32))
    out = pl.pallas_call(
        functools.partial(_ssd_kernel, nc=nc, has_init=has_init, want_final=want_final),
        grid=(batch, SSD_GROUPS),
        in_specs=in_specs,
        out_specs=out_specs,
        out_shape=out_shape,
        scratch_shapes=[pltpu.VMEM((2, SSD_D_STATE, GROUP_W), F32), pltpu.VMEM((seq, GROUP_W), F32)],
        compiler_params=_params(("parallel", "parallel"), 48),
        name="ssd_scan",
    )(*args)
    return out if want_final else (out[0], None)


def _trunk(x, pos, batch, seq, per_seq_cond, init, mod5, wts, *, want_final):
    (ln_g, ln_b, fno_w_in, fno_w_out, ssd_w_zx, ssd_w_dt, conv_w, conv_b, dt_bias, a_log,
     d_skip, norm_w, ssd_w_out) = wts
    fno_width = fno_w_out.shape[0]
    ssd_width = ssd_w_out.shape[0]
    tm_pos, tm_in, tm_out, tm_gated = 512, 1024, 512, 256

    def cond_row(tm):
        if not per_seq_cond:
            return lambda i: CTX_ROW
        assert seq % tm == 0
        return lambda i: (i * tm) // seq

    tm0 = tm_in if pos is None else tm_pos
    uz = _ln_in_proj(x, pos, mod5, 0, cond_row(tm0), fno_w_in, None, tm=tm0, tn=2048)
    yg = _fno_core(uz, batch, seq, fno_width, tw=min(fno_width, 4096 * 256 // seq))
    x1 = _out_proj_ln(yg, None, None, fno_w_out, x, pos, mod5, 0, cond_row(tm_out), ln_g, ln_b, tm=tm_out)

    zx, dt_raw = _ln_in_proj(x1, None, mod5, 1, cond_row(tm_in), ssd_w_zx, ssd_w_dt, tm=tm_in, tn=2048)
    xbc = _conv_silu(zx, conv_w, conv_b, batch, seq, ssd_width, tc=1024)
    cols_g, rows_g = _dt_prep(dt_raw, dt_bias, a_log, n_ch=8)
    y, fin = _ssd_scan(xbc, cols_g, rows_g, d_skip, init, batch, seq, want_final=want_final)
    x2 = _out_proj_ln(y, zx, norm_w, ssd_w_out, x1, None, mod5, 1, cond_row(tm_gated), ln_g, ln_b, tm=tm_gated)
    return x2, fin


def _sincos(pos, dim):
    omega = 1.0 / (10000.0 ** (np.arange(dim // 2, dtype=np.float64) / (dim / 2)))
    ang = pos.astype(np.float64)[:, None] * omega[None, :]
    return np.concatenate([np.sin(ang), np.cos(ang)], axis=-1)


def _grid_pos_embed(n_tokens, dim):
    t = np.arange(n_tokens)
    return np.concatenate([_sincos(t // GRID_W, dim // 2), _sincos(t % GRID_W, dim // 2)], axis=-1)


def kernel(x_prompt, x_sample, state_ssd_ctx, c, c_ctx, w_ada, b_ada, ln_g, ln_b, fno_w_in, fno_w_out,
           ssd_w_in, ssd_conv_w, ssd_conv_b, ssd_dt_bias, ssd_a_log, ssd_d, ssd_norm_w, ssd_w_out):
    batch, seq, d = x_prompt.shape
    dec_batch, dec_seq, _ = x_sample.shape
    ssd_width = ssd_w_out.shape[1]
    conv_dim = ssd_conv_w.shape[2]
    assert dec_batch <= CTX_ROW

    cond = jnp.zeros((COND_ROWS, d), F32).at[:dec_batch].set(c).at[CTX_ROW].set(c_ctx)
    mod = _modulation(cond, w_ada, b_ada)
    mod5 = mod.reshape(DEPTH, COND_ROWS, 3, 1, d)

    w_in = ssd_w_in[0]
    wts = (ln_g, ln_b,
           fno_w_in[0].astype(BF16), fno_w_out[0].astype(BF16),
           w_in[:, :ssd_width + conv_dim].astype(BF16), w_in[:, ssd_width + conv_dim:].astype(BF16),
           ssd_conv_w[0], ssd_conv_b[0], ssd_dt_bias[0], ssd_a_log[0],
           jnp.repeat(ssd_d[0], SSD_HEAD_DIM).reshape(1, ssd_width),
           ssd_norm_w[0], ssd_w_out[0].astype(BF16))

    y_prompt, fin = _trunk(x_prompt.reshape(batch * seq, d), None, batch, seq,
                           False, None, mod5, wts, want_final=True)
    pos = jnp.asarray(_grid_pos_embed(dec_seq, d), dtype=F32)
    y_sample, _ = _trunk(x_sample.reshape(dec_batch * dec_seq, d), pos, dec_batch, dec_seq,
                         True, state_ssd_ctx[:, 0], mod5, wts, want_final=False)
    return (y_prompt.reshape(batch, seq, d), y_sample.reshape(dec_batch, dec_seq, d),
            fin[:, None])
```

```python
import functools
import math

import numpy as np
import jax
import jax.numpy as jnp
from jax import lax
from jax.experimental import pallas as pl
from jax.experimental.pallas import tpu as pltpu

F32 = jnp.float32
BF16 = jnp.bfloat16

DEPTH = 2
GRID_W = 64
FNO_GROUP_DIM = 256
SSD_HEAD_DIM = 64
SSD_GROUPS = 8
SSD_D_STATE = 128
SSD_CONV_W = 5
CHUNK = 128
DEEPNORM_ALPHA = (2 * DEPTH) ** 0.25
LN_EPS = 1e-5
COND_ROWS = 16
CTX_ROW = 8
LANES = 128
MIB = 1024 * 1024


def _params(semantics, vmem_mib):
    return pltpu.CompilerParams(dimension_semantics=semantics, vmem_limit_bytes=vmem_mib * MIB)


def _silu(x):
    return x * jax.nn.sigmoid(x)


def _dot(a, b):
    return jnp.dot(a, b, preferred_element_type=F32)


def _mod_kernel(c_ref, w_ref, b_ref, o_ref):
    s = _silu(c_ref[...]).astype(BF16)
    o_ref[...] = _dot(s, w_ref[...].astype(BF16)) + b_ref[...]


def _modulation(cond, w_ada, b_ada):
    depth, d, n = w_ada.shape
    tn = 512
    return pl.pallas_call(
        _mod_kernel,
        grid=(depth, n // tn),
        in_specs=[
            pl.BlockSpec((COND_ROWS, d), lambda i, j: (0, 0)),
            pl.BlockSpec((None, d, tn), lambda i, j: (i, 0, j)),
            pl.BlockSpec((None, 1, tn), lambda i, j: (i, 0, j)),
        ],
        out_specs=pl.BlockSpec((None, COND_ROWS, tn), lambda i, j: (i, 0, j)),
        out_shape=jax.ShapeDtypeStruct((depth, COND_ROWS, n), F32),
        compiler_params=_params(("parallel", "parallel"), 32),
        name="adaln_mod",
    )(cond, w_ada, b_ada.reshape(depth, 1, n))


LN_ROWS = 256


def _ln_modulate(x, shift, scale):
    mu = jnp.mean(x, axis=-1, keepdims=True)
    xc = x - mu
    var = jnp.mean(xc * xc, axis=-1, keepdims=True)
    return xc * lax.rsqrt(var + LN_EPS) * (1.0 + scale) + shift


def _ln_mm_kernel(*refs, has_pos, has_dt):
    refs = list(refs)
    x_ref = refs.pop(0)
    pos_ref = refs.pop(0) if has_pos else None
    shift_ref, scale_ref, w_ref = refs.pop(0), refs.pop(0), refs.pop(0)
    wdt_ref = refs.pop(0) if has_dt else None
    o_ref = refs.pop(0)
    dt_ref = refs.pop(0) if has_dt else None
    h_scr = refs.pop(0)

    @pl.when(pl.program_id(1) == 0)
    def _():
        piece = min(LN_ROWS, x_ref.shape[0])
        for r in range(0, x_ref.shape[0], piece):
            rows = slice(r, r + piece)
            x = x_ref[rows, :]
            if has_pos:
                x = x + pos_ref[rows, :]
            h = _ln_modulate(x, shift_ref[...], scale_ref[...]).astype(BF16)
            h_scr[rows, :] = h
            if has_dt:
                dt_ref[rows, :] = _dot(h, wdt_ref[...])

    o_ref[...] = _dot(h_scr[...], w_ref[...]).astype(o_ref.dtype)


def _ln_in_proj(x, pos, mod5, layer, cond_row, w, with_dt, *, tm, tn):
    t, d = x.shape
    has_pos, has_dt = pos is not None, with_dt
    n = w.shape[1] - (LANES if has_dt else 0)
    in_specs = [pl.BlockSpec((tm, d), lambda i, j: (i, 0))]
    args = [x]
    if has_pos:
        pos_blocks = pos.shape[0] // tm
        mode = dict(pipeline_mode=pl.Buffered(1)) if pos_blocks == 1 else {}
        in_specs.append(pl.BlockSpec((tm, d), lambda i, j: (i % pos_blocks, 0), **mode))
        args.append(pos)
    for part in (0, 1):
        in_specs.append(pl.BlockSpec((None, None, None, 1, d),
                                     lambda i, j, part=part: (layer, cond_row(i), part, 0, 0)))
        args.append(mod5)
    in_specs.append(pl.BlockSpec((d, tn), lambda i, j: (0, j)))
    args.append(w)
    out_specs = [pl.BlockSpec((tm, tn), lambda i, j: (i, j))]
    out_shape = [jax.ShapeDtypeStruct((t, n), BF16)]
    if has_dt:
        in_specs.append(pl.BlockSpec((d, LANES), lambda i, j: (0, n // LANES)))
        args.append(w)
        out_specs.append(pl.BlockSpec((tm, LANES), lambda i, j: (i, 0)))
        out_shape.append(jax.ShapeDtypeStruct((t, LANES), F32))
    out = pl.pallas_call(
        functools.partial(_ln_mm_kernel, has_pos=has_pos, has_dt=has_dt),
        grid=(t // tm, n // tn),
        in_specs=in_specs,
        out_specs=out_specs,
        out_shape=out_shape,
        scratch_shapes=[pltpu.VMEM((tm, d), BF16)],
        compiler_params=_params(("parallel", "arbitrary"), 56),
        name="ln_in_proj",
    )(*args)
    return out if has_dt else out[0]


def _dft_matrices(seq, group_dim):
    def cos_sin(n):
        k = np.arange(n, dtype=np.int64)
        ang = 2.0 * np.pi * ((k[:, None] * k[None, :]) % n).astype(np.float64) / n
        return np.cos(ang), np.sin(ang)
    cl, sl = cos_sin(seq)
    cc, sc = cos_sin(group_dim)
    norm = 1.0 / math.sqrt(seq * group_dim)
    chan = np.concatenate([cc, sc], axis=1)
    posm = np.concatenate([cl * norm, -sl * norm], axis=1)
    return jnp.asarray(chan, dtype=F32).astype(BF16), jnp.asarray(posm, dtype=F32).astype(BF16)


def _fno_kernel(u_ref, z_ref, chan_ref, posm_ref, o_ref):
    seq, tw = u_ref.shape
    cg = FNO_GROUP_DIM
    for k in range(tw // cg):
        cols = slice(k * cg, (k + 1) * cg)
        t = _dot(u_ref[:, cols], chan_ref[...])
        y = (_dot(posm_ref[:, :seq], t[:, :cg].astype(BF16))
             + _dot(posm_ref[:, seq:], t[:, cg:].astype(BF16)))
        o_ref[:, cols] = (y * _silu(z_ref[:, cols].astype(F32))).astype(o_ref.dtype)


def _fno_core(uz, batch, seq, width, *, tw):
    chan, posm = _dft_matrices(seq, FNO_GROUP_DIM)
    nw = width // tw
    return pl.pallas_call(
        _fno_kernel,
        grid=(batch, nw),
        in_specs=[
            pl.BlockSpec((seq, tw), lambda b, j: (b, j)),
            pl.BlockSpec((seq, tw), lambda b, j: (b, nw + j)),
            pl.BlockSpec(chan.shape, lambda b, j: (0, 0)),
            pl.BlockSpec(posm.shape, lambda b, j: (0, 0)),
        ],
        out_specs=pl.BlockSpec((seq, tw), lambda b, j: (b, j)),
        out_shape=jax.ShapeDtypeStruct((batch * seq, width), BF16),
        compiler_params=_params(("parallel", "parallel"), 48),
        name="fno_core",
    )(uz, uz, chan, posm)


def _out_ln_kernel(*refs, has_pos, gated):
    refs = list(refs)
    a_ref = refs.pop(0)
    z_ref, nw_ref = (refs.pop(0), refs.pop(0)) if gated else (None, None)
    w_ref, x_ref = refs.pop(0), refs.pop(0)
    pos_ref = refs.pop(0) if has_pos else None
    gate_ref, g_ref, b_ref, o_ref = refs

    if gated:
        y = a_ref[...].astype(F32) * _silu(z_ref[...].astype(F32))
        ms = jnp.mean(y * y, axis=-1, keepdims=True)
        a = (y * lax.rsqrt(ms + LN_EPS) * nw_ref[...]).astype(BF16)
    else:
        a = a_ref[...]
    x = x_ref[...]
    if has_pos:
        x = x + pos_ref[...]
    r = DEEPNORM_ALPHA * x + gate_ref[...] * _dot(a, w_ref[...])
    mu = jnp.mean(r, axis=-1, keepdims=True)
    rc = r - mu
    var = jnp.mean(rc * rc, axis=-1, keepdims=True)
    o_ref[...] = rc * lax.rsqrt(var + LN_EPS) * g_ref[...] + b_ref[...]


def _out_proj_ln(a, z, norm_w, w, x, pos, mod5, layer, cond_row, ln_g, ln_b, *, tm):
    t, kdim = a.shape
    d = w.shape[1]
    has_pos, gated = pos is not None, z is not None
    row_k = pl.BlockSpec((tm, kdim), lambda i: (i, 0))
    row_d = pl.BlockSpec((tm, d), lambda i: (i, 0))
    in_specs, args = [row_k], [a]
    if gated:
        in_specs += [row_k, pl.BlockSpec((1, kdim), lambda i: (0, 0))]
        args += [z, norm_w.reshape(1, kdim)]
    in_specs += [pl.BlockSpec((kdim, d), lambda i: (0, 0), pipeline_mode=pl.Buffered(1)), row_d]
    args += [w, x]
    if has_pos:
        pos_blocks = pos.shape[0] // tm
        in_specs.append(pl.BlockSpec((tm, d), lambda i: (i % pos_blocks, 0)))
        args.append(pos)
    in_specs.append(pl.BlockSpec((None, None, None, 1, d), lambda i: (layer, cond_row(i), 2, 0, 0)))
    args.append(mod5)
    for v in (ln_g, ln_b):
        in_specs.append(pl.BlockSpec((None, 1, d), lambda i: (layer, 0, 0)))
        args.append(v.reshape(v.shape[0], 1, d))
    return pl.pallas_call(
        functools.partial(_out_ln_kernel, has_pos=has_pos, gated=gated),
        grid=(t // tm,),
        in_specs=in_specs,
        out_specs=row_d,
        out_shape=jax.ShapeDtypeStruct((t, d), F32),
        compiler_params=_params(("parallel",), 56),
        name="out_proj_ln",
    )(*args)


CONV_OFFSETS = (-2, -1, 1, 2)
CONV_PAD = 16
CONV_STRIP = 256


def _shift_matrix():
    s = np.zeros((len(CONV_OFFSETS) * CHUNK, CHUNK + 2 * CONV_PAD), np.float32)
    for t, off in enumerate(CONV_OFFSETS):
        s[t * CHUNK + np.arange(CHUNK), CONV_PAD + np.arange(CHUNK) + off] = 1.0
    return jnp.asarray(s, dtype=BF16)


def _conv_kernel(x_ref, s_ref, w_ref, b_ref, o_ref, pad_scr):
    seq, tc = x_ref.shape
    pad_scr[0:CONV_PAD, :] = jnp.zeros((CONV_PAD, tc), BF16)
    pad_scr[CONV_PAD + seq:2 * CONV_PAD + seq, :] = jnp.zeros((CONV_PAD, tc), BF16)
    pad_scr[CONV_PAD:CONV_PAD + seq, :] = x_ref[...]
    centre = SSD_CONV_W // 2
    for i in range(seq // CHUNK):
        for c in range(0, tc, CONV_STRIP):
            lanes = slice(c, c + CONV_STRIP)
            window = pad_scr[i * CHUNK:(i + 1) * CHUNK + 2 * CONV_PAD, lanes]
            taps = _dot(s_ref[...], window)
            acc = (b_ref[:, lanes]
                   + window[CONV_PAD:CONV_PAD + CHUNK, :].astype(F32) * w_ref[centre:centre + 1, lanes])
            for t, off in enumerate(CONV_OFFSETS):
                acc = acc + taps[t * CHUNK:(t + 1) * CHUNK, :] * w_ref[centre + off:centre + off + 1, lanes]
            o_ref[i * CHUNK:(i + 1) * CHUNK, lanes] = _silu(acc).astype(o_ref.dtype)


def _conv_silu(zx, conv_w, conv_b, batch, seq, col0, *, tc):
    conv_dim = conv_w.shape[1]
    assert col0 % tc == 0 and conv_dim % tc == 0
    c0 = col0 // tc
    w8 = jnp.zeros((8, conv_dim), F32).at[:SSD_CONV_W].set(conv_w)
    shift = _shift_matrix()
    return pl.pallas_call(
        _conv_kernel,
        grid=(batch, conv_dim // tc),
        in_specs=[pl.BlockSpec((seq, tc), lambda b, j: (b, c0 + j)),
                  pl.BlockSpec(shift.shape, lambda b, j: (0, 0)),
                  pl.BlockSpec((8, tc), lambda b, j: (0, j)),
                  pl.BlockSpec((1, tc), lambda b, j: (0, j))],
        out_specs=pl.BlockSpec((seq, tc), lambda b, j: (b, j)),
        out_shape=jax.ShapeDtypeStruct((batch * seq, conv_dim), BF16),
        scratch_shapes=[pltpu.VMEM((seq + 2 * CONV_PAD, tc), BF16)],
        compiler_params=_params(("parallel", "parallel"), 48),
        name="ssd_conv_silu",
    )(zx, shift, w8, conv_b.reshape(1, conv_dim))


def _split3(x):
    hi = x.astype(BF16)
    r = x - hi.astype(F32)
    mid = r.astype(BF16)
    lo = (r - mid.astype(F32)).astype(BF16)
    return hi, mid, lo


HEADS_PER_GROUP = 8
PAIRS = HEADS_PER_GROUP // 2
GROUP_W = HEADS_PER_GROUP * SSD_HEAD_DIM
N_DH = 2 * HEADS_PER_GROUP
SPLIT = 3
E_COL0 = N_DH * SPLIT
LOG2E = 1.4426950408889634


def _group_select_matrix():
    sel = np.zeros((2 * SPLIT * LANES, SSD_GROUPS * LANES), np.float32)
    half = LANES // 2
    for g in range(SSD_GROUPS):
        for j in range(N_DH):
            d, h = divmod(j, HEADS_PER_GROUP)
            src = d * half + g * HEADS_PER_GROUP + h
            for p in range(SPLIT):
                sel[p * LANES + src, g * LANES + SPLIT * j + p] = 1.0
                sel[(SPLIT + p) * LANES + src, g * LANES + E_COL0 + SPLIT * j + p] = 1.0
    return jnp.asarray(sel, dtype=BF16)


def _lane_bcast_matrices():
    e = np.zeros((2, LANES, (HEADS_PER_GROUP + PAIRS) * LANES), np.float32)
    for d in range(2):
        for h in range(HEADS_PER_GROUP):
            j = d * HEADS_PER_GROUP + h
            e[d, SPLIT * j:SPLIT * (j + 1), h * LANES:(h + 1) * LANES] = 1.0
            lane0 = (HEADS_PER_GROUP + h // 2) * LANES + (h % 2) * SSD_HEAD_DIM
            e[d, E_COL0 + SPLIT * j:E_COL0 + SPLIT * (j + 1), lane0:lane0 + SSD_HEAD_DIM] = 1.0
    return jnp.asarray(e, dtype=BF16)


def _dt_kernel(raw_ref, bias_ref, alog_ref, sel_ref, col_ref, row_ref):
    n_ch = row_ref.shape[1]
    half = LANES // 2
    row = lax.broadcasted_iota(jnp.int32, (CHUNK, CHUNK), 0)
    col = lax.broadcasted_iota(jnp.int32, (CHUNK, CHUNK), 1)
    tri_f = jnp.where(row >= col, 1.0, 0.0).astype(BF16)
    tri_b = jnp.where(row <= col, 1.0, 0.0).astype(BF16)
    is_fwd = col < half
    a2 = -jnp.exp(alog_ref[...]) * LOG2E
    for k in range(n_ch):
        rows = slice(k * CHUNK, (k + 1) * CHUNK)
        x = raw_ref[rows, :] + bias_ref[...]
        dt = jnp.maximum(x, 0.0) + jnp.log1p(jnp.exp(-jnp.abs(x)))
        parts = _split3(dt * a2)
        cs = jnp.where(is_fwd, sum(_dot(tri_f, p) for p in parts), sum(_dot(tri_b, p) for p in parts))
        tot = jnp.where(is_fwd[:1], cs[CHUNK - 1:CHUNK, :], cs[0:1, :])
        w = dt * jnp.exp2(tot - cs)
        pieces = jnp.concatenate(list(_split3(cs)) + list(_split3(jnp.exp2(cs))), axis=1)
        grouped = _dot(pieces, sel_ref[...]).astype(BF16)
        transposed = (cs.T, dt.T, w.T)
        for g in range(SSD_GROUPS):
            col_ref[g, rows, :] = grouped[:, g * LANES:(g + 1) * LANES]
            for n, v in enumerate(transposed):
                for d in (0, 1):
                    src = d * half + g * HEADS_PER_GROUP
                    dst = n * N_DH + d * HEADS_PER_GROUP
                    row_ref[g, k, dst:dst + HEADS_PER_GROUP, :] = v[src:src + HEADS_PER_GROUP, :]


def _dt_prep(dt_raw, dt_bias, a_log, *, n_ch):
    t = dt_raw.shape[0]
    tm = n_ch * CHUNK
    sel = _group_select_matrix()
    vec_spec = pl.BlockSpec((1, LANES), lambda i: (0, 0))
    return pl.pallas_call(
        _dt_kernel,
        grid=(t // tm,),
        in_specs=[pl.BlockSpec((tm, LANES), lambda i: (i, 0)), vec_spec, vec_spec,
                  pl.BlockSpec(sel.shape, lambda i: (0, 0))],
        out_specs=[pl.BlockSpec((SSD_GROUPS, tm, LANES), lambda i: (0, i, 0)),
                   pl.BlockSpec((SSD_GROUPS, n_ch, 3 * N_DH, CHUNK), lambda i: (0, i, 0, 0))],
        out_shape=[jax.ShapeDtypeStruct((SSD_GROUPS, t, LANES), BF16),
                   jax.ShapeDtypeStruct((SSD_GROUPS, t // CHUNK, 3 * N_DH, CHUNK), F32)],
        compiler_params=_params(("parallel",), 32),
        name="ssd_dt_prep",
    )(dt_raw, dt_bias.reshape(1, LANES), a_log.reshape(1, LANES), sel)


def _ssd_kernel(*refs, nc, has_init, want_final):
    refs = list(refs)
    xs_ref, b_ref, c_ref, col_ref, row_ref, bcast_ref, dsk_ref = refs[:7]
    del refs[:7]
    init_ref = refs.pop(0) if has_init else None
    y_ref = refs.pop(0)
    fin_ref = refs.pop(0) if want_final else None
    st_scr, y_scr = refs

    p = SSD_HEAD_DIM
    row = lax.broadcasted_iota(jnp.int32, (CHUNK, CHUNK), 0)
    col = lax.broadcasted_iota(jnp.int32, (CHUNK, CHUNK), 1)
    lo_half = col < p

    for d in (0, 1):
        for q in range(PAIRS):
            lanes = slice(q * LANES, (q + 1) * LANES)
            if has_init:
                blk = init_ref[d, 2 * q:2 * q + 2].reshape(2 * p, SSD_D_STATE)
                st_scr[d, :, lanes] = blk.T
            else:
                st_scr[d, :, lanes] = jnp.zeros((SSD_D_STATE, LANES), F32)
    def chunk_body(i, carry, first):
        for d in (0, 1):
            causal = (row >= col) if d == 0 else (row <= col)
            c = i if d == 0 else nc - 1 - i
            r0 = pl.multiple_of(c * CHUNK, CHUNK)
            rows = pl.ds(r0, CHUNK)
            bc = b_ref[rows, :]
            cc = c_ref[rows, :]
            g = lax.dot_general(cc, bc, (((1,), (1,)), ((), ())), preferred_element_type=F32)
            gm = jnp.where(causal, g, 0.0)
            bt = bc.astype(F32).T
            cols = col_ref[rows, :]
            edge = CHUNK - 1 if d == 0 else 0
            y_off = _dot(cc, st_scr[d].astype(BF16))
            spread = _dot(cols, bcast_ref[d])
            xbd_q, m_q, bt_q = [], [], []
            for q in range(PAIRS):
                x = xs_ref[rows, q * LANES:(q + 1) * LANES]
                zero = jnp.zeros_like(x)
                xbd_q.append(jnp.concatenate([jnp.where(lo_half, x, zero), jnp.where(lo_half, zero, x)], axis=0))
                m_h, bt_h = [], []
                for h in (2 * q, 2 * q + 1):
                    j = d * HEADS_PER_GROUP + h
                    cs_l = spread[:, h * LANES:(h + 1) * LANES]
                    cs_s = row_ref[c, pl.ds(j, 1), :]
                    dt_s = row_ref[c, pl.ds(N_DH + j, 1), :]
                    w_s = row_ref[c, pl.ds(2 * N_DH + j, 1), :]
                    lmat = jnp.exp2(jnp.minimum(cs_l - cs_s, 0.0))
                    m_h.append((gm * lmat * dt_s).astype(BF16))
                    bt_h.append((bt * w_s).astype(BF16))
                m_q.append(jnp.concatenate(m_h, axis=1))
                bt_q.append(jnp.concatenate(bt_h, axis=1))
            y_diag = [_dot(m_q[q], xbd_q[q]) for q in range(PAIRS)]
            s_new = [_dot(bt_q[q], xbd_q[q]) for q in range(PAIRS)]
            for q in range(PAIRS):
                lanes = slice(q * LANES, (q + 1) * LANES)
                e_q = spread[:, (HEADS_PER_GROUP + q) * LANES:(HEADS_PER_GROUP + q + 1) * LANES]
                y = y_diag[q] + y_off[:, lanes] * e_q
                if first:
                    y_scr[rows, lanes] = y + dsk_ref[:, lanes] * xs_ref[rows, lanes].astype(F32)
                else:
                    y_ref[rows, lanes] = (y_scr[rows, lanes] + y).astype(y_ref.dtype)
                st_scr[d, :, lanes] = st_scr[d, :, lanes] * e_q[edge:edge + 1, :] + s_new[q]
        return carry

    assert nc % 2 == 0
    lax.fori_loop(0, nc // 2, functools.partial(chunk_body, first=True), 0)
    lax.fori_loop(nc // 2, nc, functools.partial(chunk_body, first=False), 0)

    if want_final:
        for d in (0, 1):
            for q in range(PAIRS):
                lanes = slice(q * LANES, (q + 1) * LANES)
                fin_ref[d, 2 * q:2 * q + 2] = st_scr[d, :, lanes].T.reshape(2, p, SSD_D_STATE)


def _ssd_scan(xbc, cols_g, rows_g, d_skip, init, batch, seq, *, want_final):
    nc = seq // CHUNK
    width = SSD_GROUPS * GROUP_W
    heads = SSD_GROUPS * HEADS_PER_GROUP
    has_init = init is not None
    bcast = _lane_bcast_matrices()
    in_specs = [
        pl.BlockSpec((seq, GROUP_W), lambda b, g: (b, g)),
        pl.BlockSpec((seq, SSD_D_STATE), lambda b, g: (b, width // SSD_D_STATE + g)),
        pl.BlockSpec((seq, SSD_D_STATE), lambda b, g: (b, width // SSD_D_STATE + SSD_GROUPS + g)),
        pl.BlockSpec((None, seq, LANES), lambda b, g: (g, b, 0)),
        pl.BlockSpec((None, nc, 3 * N_DH, CHUNK), lambda b, g: (g, b, 0, 0)),
        pl.BlockSpec(bcast.shape, lambda b, g: (0, 0, 0)),
        pl.BlockSpec((1, GROUP_W), lambda b, g: (0, g)),
    ]
    args = [xbc, xbc, xbc, cols_g, rows_g, bcast, d_skip]
    state_spec = pl.BlockSpec((None, 2, HEADS_PER_GROUP, SSD_HEAD_DIM, SSD_D_STATE),
                              lambda b, g: (b, 0, g, 0, 0))
    if has_init:
        in_specs.append(state_spec)
        args.append(init)
    out_specs = [pl.BlockSpec((seq, GROUP_W), lambda b, g: (b, g))]
    out_shape = [jax.ShapeDtypeStruct((batch * seq, width), BF16)]
    if want_final:
        out_specs.append(state_spec)
        out_shape.append(jax.ShapeDtypeStruct((batch, 2, heads, SSD_HEAD_DIM, SSD_D_STATE), F32))
    out = pl.pallas_call(
        functools.partial(_ssd_kernel, nc=nc, has_init=has_init, want_final=want_final),
        grid=(batch, SSD_GROUPS),
        in_specs=in_specs,
        out_specs=out_specs,
        out_shape=out_shape,
        scratch_shapes=[pltpu.VMEM((2, SSD_D_STATE, GROUP_W), F32), pltpu.VMEM((seq, GROUP_W), F32)],
        compiler_params=_params(("parallel", "parallel"), 48),
        name="ssd_scan",
    )(*args)
    return out if want_final else (out[0], None)


def _trunk(x, pos, batch, seq, per_seq_cond, init, mod5, wts, *, want_final):
    (ln_g, ln_b, fno_w_in, fno_w_out, ssd_w_in, conv_w, conv_b, dt_bias, a_log,
     d_skip, norm_w, ssd_w_out) = wts
    fno_width = fno_w_out.shape[0]
    ssd_width = ssd_w_out.shape[0]
    tm_in, tm_out, tm_gated = 1024, 512, 256

    def cond_row(tm):
        if not per_seq_cond:
            return lambda i: CTX_ROW
        assert seq % tm == 0
        return lambda i: (i * tm) // seq

    tn0 = 2048 if pos is None else 1024
    uz = _ln_in_proj(x, pos, mod5, 0, cond_row(tm_in), fno_w_in, False, tm=tm_in, tn=tn0)
    yg = _fno_core(uz, batch, seq, fno_width, tw=min(fno_width, 4096 * 256 // seq))
    x1 = _out_proj_ln(yg, None, None, fno_w_out, x, pos, mod5, 0, cond_row(tm_out), ln_g, ln_b, tm=tm_out)

    zx, dt_raw = _ln_in_proj(x1, None, mod5, 1, cond_row(tm_in), ssd_w_in, True, tm=tm_in, tn=2048)
    conv_tc = 1024 if seq >= 1024 else 2048
    xbc = _conv_silu(zx, conv_w, conv_b, batch, seq, ssd_width, tc=conv_tc)
    cols_g, rows_g = _dt_prep(dt_raw, dt_bias, a_log, n_ch=8)
    y, fin = _ssd_scan(xbc, cols_g, rows_g, d_skip, init, batch, seq, want_final=want_final)
    x2 = _out_proj_ln(y, zx, norm_w, ssd_w_out, x1, None, mod5, 1, cond_row(tm_gated), ln_g, ln_b, tm=tm_gated)
    return x2, fin


def _sincos(pos, dim):
    omega = 1.0 / (10000.0 ** (np.arange(dim // 2, dtype=np.float64) / (dim / 2)))
    ang = pos.astype(np.float64)[:, None] * omega[None, :]
    return np.concatenate([np.sin(ang), np.cos(ang)], axis=-1)


def _grid_pos_embed(n_tokens, dim):
    t = np.arange(n_tokens)
    return np.concatenate([_sincos(t // GRID_W, dim // 2), _sincos(t % GRID_W, dim // 2)], axis=-1)


def kernel(x_prompt, x_sample, state_ssd_ctx, c, c_ctx, w_ada, b_ada, ln_g, ln_b, fno_w_in, fno_w_out,
           ssd_w_in, ssd_conv_w, ssd_conv_b, ssd_dt_bias, ssd_a_log, ssd_d, ssd_norm_w, ssd_w_out):
    batch, seq, d = x_prompt.shape
    dec_batch, dec_seq, _ = x_sample.shape
    ssd_width = ssd_w_out.shape[1]
    conv_dim = ssd_conv_w.shape[2]
    assert dec_batch <= CTX_ROW

    cond = jnp.zeros((COND_ROWS, d), F32).at[:dec_batch].set(c).at[CTX_ROW].set(c_ctx)
    mod = _modulation(cond, w_ada, b_ada)
    mod5 = mod.reshape(DEPTH, COND_ROWS, 3, 1, d)

    assert ssd_w_in.shape[2] == ssd_width + conv_dim + LANES
    wts = (ln_g, ln_b,
           fno_w_in[0].astype(BF16), fno_w_out[0].astype(BF16), ssd_w_in[0].astype(BF16),
           ssd_conv_w[0], ssd_conv_b[0], ssd_dt_bias[0], ssd_a_log[0],
           jnp.repeat(ssd_d[0], SSD_HEAD_DIM).reshape(1, ssd_width),
           ssd_norm_w[0], ssd_w_out[0].astype(BF16))

    y_prompt, fin = _trunk(x_prompt.reshape(batch * seq, d), None, batch, seq,
                           False, None, mod5, wts, want_final=True)
    pos = jnp.asarray(_grid_pos_embed(dec_seq, d), dtype=F32)
    y_sample, _ = _trunk(x_sample.reshape(dec_batch * dec_seq, d), pos, dec_batch, dec_seq,
                         True, state_ssd_ctx[:, 0], mod5, wts, want_final=False)
    return (y_prompt.reshape(batch, seq, d), y_sample.reshape(dec_batch, dec_seq, d),
            fin[:, None])
```

```python
import functools
import math

import numpy as np
import jax
import jax.numpy as jnp
from jax import lax
from jax.experimental import pallas as pl
from jax.experimental.pallas import tpu as pltpu

F32 = jnp.float32
BF16 = jnp.bfloat16

DEPTH = 2
GRID_W = 64
FNO_GROUP_DIM = 256
SSD_HEAD_DIM = 64
SSD_GROUPS = 8
SSD_D_STATE = 128
SSD_CONV_W = 5
CHUNK = 128
DEEPNORM_ALPHA = (2 * DEPTH) ** 0.25
LN_EPS = 1e-5
COND_ROWS = 16
CTX_ROW = 8
LANES = 128
MIB = 1024 * 1024


def _params(semantics, vmem_mib):
    return pltpu.CompilerParams(dimension_semantics=semantics, vmem_limit_bytes=vmem_mib * MIB)


def _silu(x):
    return x * jax.nn.sigmoid(x)


def _dot(a, b):
    return jnp.dot(a, b, preferred_element_type=F32)


def _mod_kernel(c_ref, w_ref, b_ref, o_ref):
    s = _silu(c_ref[...]).astype(BF16)
    o_ref[...] = _dot(s, w_ref[...].astype(BF16)) + b_ref[...]


def _modulation(cond, w_ada, b_ada):
    depth, d, n = w_ada.shape
    tn = 512
    return pl.pallas_call(
        _mod_kernel,
        grid=(depth, n // tn),
        in_specs=[
            pl.BlockSpec((COND_ROWS, d), lambda i, j: (0, 0)),
            pl.BlockSpec((None, d, tn), lambda i, j: (i, 0, j)),
            pl.BlockSpec((None, 1, tn), lambda i, j: (i, 0, j)),
        ],
        out_specs=pl.BlockSpec((None, COND_ROWS, tn), lambda i, j: (i, 0, j)),
        out_shape=jax.ShapeDtypeStruct((depth, COND_ROWS, n), F32),
        compiler_params=_params(("parallel", "parallel"), 32),
        name="adaln_mod",
    )(cond, w_ada, b_ada.reshape(depth, 1, n))


LN_ROWS = 256


def _ln_modulate(x, shift, scale):
    mu = jnp.mean(x, axis=-1, keepdims=True)
    xc = x - mu
    var = jnp.mean(xc * xc, axis=-1, keepdims=True)
    return xc * lax.rsqrt(var + LN_EPS) * (1.0 + scale) + shift


def _ln_mm_kernel(*refs, has_pos, has_dt):
    refs = list(refs)
    x_ref = refs.pop(0)
    pos_ref = refs.pop(0) if has_pos else None
    shift_ref, scale_ref, w_ref = refs.pop(0), refs.pop(0), refs.pop(0)
    wdt_ref = refs.pop(0) if has_dt else None
    o_ref = refs.pop(0)
    dt_ref = refs.pop(0) if has_dt else None
    h_scr = refs.pop(0)

    @pl.when(pl.program_id(1) == 0)
    def _():
        piece = min(LN_ROWS, x_ref.shape[0])
        for r in range(0, x_ref.shape[0], piece):
            rows = slice(r, r + piece)
            x = x_ref[rows, :]
            if has_pos:
                x = x + pos_ref[rows, :]
            h = _ln_modulate(x, shift_ref[...], scale_ref[...]).astype(BF16)
            h_scr[rows, :] = h
            if has_dt:
                dt_ref[rows, :] = _dot(h, wdt_ref[...])

    o_ref[...] = _dot(h_scr[...], w_ref[...]).astype(o_ref.dtype)


def _ln_in_proj(x, pos, mod5, layer, cond_row, w, with_dt, *, tm, tn):
    t, d = x.shape
    has_pos, has_dt = pos is not None, with_dt
    n = w.shape[1] - (LANES if has_dt else 0)
    in_specs = [pl.BlockSpec((tm, d), lambda i, j: (i, 0))]
    args = [x]
    if has_pos:
        pos_blocks = pos.shape[0] // tm
        mode = dict(pipeline_mode=pl.Buffered(1)) if pos_blocks == 1 else {}
        in_specs.append(pl.BlockSpec((tm, d), lambda i, j: (i % pos_blocks, 0), **mode))
        args.append(pos)
    for part in (0, 1):
        in_specs.append(pl.BlockSpec((None, None, None, 1, d),
                                     lambda i, j, part=part: (layer, cond_row(i), part, 0, 0)))
        args.append(mod5)
    in_specs.append(pl.BlockSpec((d, tn), lambda i, j: (0, j)))
    args.append(w)
    out_specs = [pl.BlockSpec((tm, tn), lambda i, j: (i, j))]
    out_shape = [jax.ShapeDtypeStruct((t, n), BF16)]
    if has_dt:
        in_specs.append(pl.BlockSpec((d, LANES), lambda i, j: (0, n // LANES)))
        args.append(w)
        out_specs.append(pl.BlockSpec((tm, LANES), lambda i, j: (i, 0)))
        out_shape.append(jax.ShapeDtypeStruct((t, LANES), F32))
    out = pl.pallas_call(
        functools.partial(_ln_mm_kernel, has_pos=has_pos, has_dt=has_dt),
        grid=(t // tm, n // tn),
        in_specs=in_specs,
        out_specs=out_specs,
        out_shape=out_shape,
        scratch_shapes=[pltpu.VMEM((tm, d), BF16)],
        compiler_params=_params(("parallel", "arbitrary"), 56),
        name="ln_in_proj",
    )(*args)
    return out if has_dt else out[0]


DFT_TAIL = 16


def _dft_matrices(seq, group_dim):
    def cos_sin(n):
        k = np.arange(n, dtype=np.int64)
        ang = 2.0 * np.pi * ((k[:, None] * k[None, :]) % n).astype(np.float64) / n
        return np.cos(ang), np.sin(ang)
    half = seq // 2
    cl, sl = cos_sin(seq)
    cc, sc = cos_sin(group_dim)
    norm = 1.0 / math.sqrt(seq * group_dim)
    posm = np.zeros((2 * half + DFT_TAIL, seq))
    posm[:half] = cl[:half] * norm
    posm[half:2 * half] = sl[:half] * norm
    posm[2 * half] = cl[half] * norm
    flip = np.zeros((half, half))
    flip[0, 0] = 1.0
    flip[np.arange(1, half), half - np.arange(1, half)] = 1.0
    as_bf16 = lambda a: jnp.asarray(a, dtype=F32).astype(BF16)
    return as_bf16(posm), as_bf16(np.stack([cc, sc])), as_bf16(flip)


def _fno_kernel(u_ref, z_ref, posm_ref, chan_ref, flip_ref, o_ref):
    seq, tw = u_ref.shape
    half = seq // 2
    cg = FNO_GROUP_DIM
    first_row = lax.broadcasted_iota(jnp.int32, (half, cg), 0) == 0
    groups = [slice(k * cg, (k + 1) * cg) for k in range(tw // cg)]
    ab = _dot(posm_ref[...], u_ref[...]).astype(BF16)
    p = [_dot(ab[:half, cols], chan_ref[0]) for cols in groups]
    q = [_dot(ab[half:2 * half, cols], chan_ref[1]) for cols in groups]
    p_nyq = [_dot(ab[2 * half:, cols], chan_ref[0])[0:1, :] for cols in groups]
    mirrored = jnp.concatenate(
        [jnp.where(first_row, p_nyq[k], p[k] + q[k]).astype(BF16) for k in range(len(groups))], axis=1)
    bottom = _dot(flip_ref[...], mirrored)
    for k, cols in enumerate(groups):
        gate = _silu(z_ref[:, cols].astype(F32))
        o_ref[:half, cols] = ((p[k] - q[k]) * gate[:half]).astype(o_ref.dtype)
        o_ref[half:, cols] = (bottom[:, cols] * gate[half:]).astype(o_ref.dtype)


def _fno_core(uz, batch, seq, width, *, tw):
    consts = _dft_matrices(seq, FNO_GROUP_DIM)
    nw = width // tw
    const_specs = [pl.BlockSpec(c.shape, lambda b, j, nd=c.ndim: (0,) * nd) for c in consts]
    return pl.pallas_call(
        _fno_kernel,
        grid=(batch, nw),
        in_specs=[
            pl.BlockSpec((seq, tw), lambda b, j: (b, j)),
            pl.BlockSpec((seq, tw), lambda b, j: (b, nw + j)),
        ] + const_specs,
        out_specs=pl.BlockSpec((seq, tw), lambda b, j: (b, j)),
        out_shape=jax.ShapeDtypeStruct((batch * seq, width), BF16),
        compiler_params=_params(("parallel", "parallel"), 48),
        name="fno_core",
    )(uz, uz, *consts)


def _out_ln_kernel(*refs, has_pos, gated):
    refs = list(refs)
    a_ref = refs.pop(0)
    z_ref, nw_ref = (refs.pop(0), refs.pop(0)) if gated else (None, None)
    w_ref, x_ref = refs.pop(0), refs.pop(0)
    pos_ref = refs.pop(0) if has_pos else None
    gate_ref, g_ref, b_ref, o_ref = refs

    if gated:
        y = a_ref[...].astype(F32) * _silu(z_ref[...].astype(F32))
        ms = jnp.mean(y * y, axis=-1, keepdims=True)
        a = (y * lax.rsqrt(ms + LN_EPS) * nw_ref[...]).astype(BF16)
    else:
        a = a_ref[...]
    x = x_ref[...]
    if has_pos:
        x = x + pos_ref[...]
    r = DEEPNORM_ALPHA * x + gate_ref[...] * _dot(a, w_ref[...])
    mu = jnp.mean(r, axis=-1, keepdims=True)
    rc = r - mu
    var = jnp.mean(rc * rc, axis=-1, keepdims=True)
    o_ref[...] = rc * lax.rsqrt(var + LN_EPS) * g_ref[...] + b_ref[...]


def _out_proj_ln(a, z, norm_w, w, x, pos, mod5, layer, cond_row, ln_g, ln_b, *, tm):
    t, kdim = a.shape
    d = w.shape[1]
    has_pos, gated = pos is not None, z is not None
    row_k = pl.BlockSpec((tm, kdim), lambda i: (i, 0))
    row_d = pl.BlockSpec((tm, d), lambda i: (i, 0))
    in_specs, args = [row_k], [a]
    if gated:
        in_specs += [row_k, pl.BlockSpec((1, kdim), lambda i: (0, 0))]
        args += [z, norm_w.reshape(1, kdim)]
    in_specs += [pl.BlockSpec((kdim, d), lambda i: (0, 0), pipeline_mode=pl.Buffered(1)), row_d]
    args += [w, x]
    if has_pos:
        pos_blocks = pos.shape[0] // tm
        in_specs.append(pl.BlockSpec((tm, d), lambda i: (i % pos_blocks, 0)))
        args.append(pos)
    in_specs.append(pl.BlockSpec((None, None, None, 1, d), lambda i: (layer, cond_row(i), 2, 0, 0)))
    args.append(mod5)
    for v in (ln_g, ln_b):
        in_specs.append(pl.BlockSpec((None, 1, d), lambda i: (layer, 0, 0)))
        args.append(v.reshape(v.shape[0], 1, d))
    return pl.pallas_call(
        functools.partial(_out_ln_kernel, has_pos=has_pos, gated=gated),
        grid=(t // tm,),
        in_specs=in_specs,
        out_specs=row_d,
        out_shape=jax.ShapeDtypeStruct((t, d), F32),
        compiler_params=_params(("parallel",), 56),
        name="out_proj_ln",
    )(*args)


CONV_OFFSETS = (-2, -1, 1, 2)
CONV_PAD = 16
CONV_STRIP = 256


def _shift_matrix():
    s = np.zeros((len(CONV_OFFSETS) * CHUNK, CHUNK + 2 * CONV_PAD), np.float32)
    for t, off in enumerate(CONV_OFFSETS):
        s[t * CHUNK + np.arange(CHUNK), CONV_PAD + np.arange(CHUNK) + off] = 1.0
    return jnp.asarray(s, dtype=BF16)


def _conv_kernel(x_ref, s_ref, w_ref, b_ref, o_ref, pad_scr):
    seq, tc = x_ref.shape
    pad_scr[0:CONV_PAD, :] = jnp.zeros((CONV_PAD, tc), BF16)
    pad_scr[CONV_PAD + seq:2 * CONV_PAD + seq, :] = jnp.zeros((CONV_PAD, tc), BF16)
    pad_scr[CONV_PAD:CONV_PAD + seq, :] = x_ref[...]
    centre = SSD_CONV_W // 2
    for i in range(seq // CHUNK):
        for c in range(0, tc, CONV_STRIP):
            lanes = slice(c, c + CONV_STRIP)
            window = pad_scr[i * CHUNK:(i + 1) * CHUNK + 2 * CONV_PAD, lanes]
            taps = _dot(s_ref[...], window)
            acc = (b_ref[:, lanes]
                   + window[CONV_PAD:CONV_PAD + CHUNK, :].astype(F32) * w_ref[centre:centre + 1, lanes])
            for t, off in enumerate(CONV_OFFSETS):
                acc = acc + taps[t * CHUNK:(t + 1) * CHUNK, :] * w_ref[centre + off:centre + off + 1, lanes]
            o_ref[i * CHUNK:(i + 1) * CHUNK, lanes] = _silu(acc).astype(o_ref.dtype)


def _conv_silu(zx, conv_w, conv_b, batch, seq, col0, *, tc):
    conv_dim = conv_w.shape[1]
    assert col0 % tc == 0 and conv_dim % tc == 0
    c0 = col0 // tc
    w8 = jnp.zeros((8, conv_dim), F32).at[:SSD_CONV_W].set(conv_w)
    shift = _shift_matrix()
    return pl.pallas_call(
        _conv_kernel,
        grid=(batch, conv_dim // tc),
        in_specs=[pl.BlockSpec((seq, tc), lambda b, j: (b, c0 + j)),
                  pl.BlockSpec(shift.shape, lambda b, j: (0, 0)),
                  pl.BlockSpec((8, tc), lambda b, j: (0, j)),
                  pl.BlockSpec((1, tc), lambda b, j: (0, j))],
        out_specs=pl.BlockSpec((seq, tc), lambda b, j: (b, j)),
        out_shape=jax.ShapeDtypeStruct((batch * seq, conv_dim), BF16),
        scratch_shapes=[pltpu.VMEM((seq + 2 * CONV_PAD, tc), BF16)],
        compiler_params=_params(("parallel", "parallel"), 48),
        name="ssd_conv_silu",
    )(zx, shift, w8, conv_b.reshape(1, conv_dim))


def _split3(x):
    hi = x.astype(BF16)
    r = x - hi.astype(F32)
    mid = r.astype(BF16)
    lo = (r - mid.astype(F32)).astype(BF16)
    return hi, mid, lo


HEADS_PER_GROUP = 8
PAIRS = HEADS_PER_GROUP // 2
GROUP_W = HEADS_PER_GROUP * SSD_HEAD_DIM
N_DH = 2 * HEADS_PER_GROUP
SPLIT = 3
LOG2E = 1.4426950408889634


def _group_select_matrix():
    sel = np.zeros((SPLIT * LANES, SSD_GROUPS * LANES), np.float32)
    half = LANES // 2
    for g in range(SSD_GROUPS):
        for j in range(N_DH):
            d, h = divmod(j, HEADS_PER_GROUP)
            src = d * half + g * HEADS_PER_GROUP + h
            for p in range(SPLIT):
                sel[p * LANES + src, g * LANES + SPLIT * j + p] = 1.0
    return jnp.asarray(sel, dtype=BF16)


def _lane_bcast_matrices():
    e = np.zeros((2, LANES, HEADS_PER_GROUP * LANES), np.float32)
    for d in range(2):
        for h in range(HEADS_PER_GROUP):
            j = d * HEADS_PER_GROUP + h
            e[d, SPLIT * j:SPLIT * (j + 1), h * LANES:(h + 1) * LANES] = 1.0
    return jnp.asarray(e, dtype=BF16)


def _dt_kernel(raw_ref, bias_ref, alog_ref, sel_ref, col_ref, row_ref):
    n_ch = row_ref.shape[1]
    half = LANES // 2
    row = lax.broadcasted_iota(jnp.int32, (CHUNK, CHUNK), 0)
    col = lax.broadcasted_iota(jnp.int32, (CHUNK, CHUNK), 1)
    tri_f = jnp.where(row >= col, 1.0, 0.0).astype(BF16)
    tri_b = jnp.where(row <= col, 1.0, 0.0).astype(BF16)
    is_fwd = col < half
    a2 = -jnp.exp(alog_ref[...]) * LOG2E
    for k in range(n_ch):
        rows = slice(k * CHUNK, (k + 1) * CHUNK)
        x = raw_ref[rows, :] + bias_ref[...]
        dt = jnp.maximum(x, 0.0) + jnp.log1p(jnp.exp(-jnp.abs(x)))
        parts = _split3(dt * a2)
        cs = jnp.where(is_fwd, sum(_dot(tri_f, p) for p in parts), sum(_dot(tri_b, p) for p in parts))
        tot = jnp.where(is_fwd[:1], cs[CHUNK - 1:CHUNK, :], cs[0:1, :])
        w = dt * jnp.exp2(tot - cs)
        pieces = jnp.concatenate(_split3(cs), axis=1)
        grouped = _dot(pieces, sel_ref[...]).astype(BF16)
        transposed = (cs.T, dt.T, w.T)
        for g in range(SSD_GROUPS):
            col_ref[g, rows, :] = grouped[:, g * LANES:(g + 1) * LANES]
            for n, v in enumerate(transposed):
                for d in (0, 1):
                    src = d * half + g * HEADS_PER_GROUP
                    dst = n * N_DH + d * HEADS_PER_GROUP
                    row_ref[g, k, dst:dst + HEADS_PER_GROUP, :] = v[src:src + HEADS_PER_GROUP, :]


def _dt_prep(dt_raw, dt_bias, a_log, *, n_ch):
    t = dt_raw.shape[0]
    tm = n_ch * CHUNK
    sel = _group_select_matrix()
    vec_spec = pl.BlockSpec((1, LANES), lambda i: (0, 0))
    return pl.pallas_call(
        _dt_kernel,
        grid=(t // tm,),
        in_specs=[pl.BlockSpec((tm, LANES), lambda i: (i, 0)), vec_spec, vec_spec,
                  pl.BlockSpec(sel.shape, lambda i: (0, 0))],
        out_specs=[pl.BlockSpec((SSD_GROUPS, tm, LANES), lambda i: (0, i, 0)),
                   pl.BlockSpec((SSD_GROUPS, n_ch, 3 * N_DH, CHUNK), lambda i: (0, i, 0, 0))],
        out_shape=[jax.ShapeDtypeStruct((SSD_GROUPS, t, LANES), BF16),
                   jax.ShapeDtypeStruct((SSD_GROUPS, t // CHUNK, 3 * N_DH, CHUNK), F32)],
        compiler_params=_params(("parallel",), 32),
        name="ssd_dt_prep",
    )(dt_raw, dt_bias.reshape(1, LANES), a_log.reshape(1, LANES), sel)


def _ssd_kernel(*refs, nc, has_init, want_final):
    refs = list(refs)
    xs_ref, b_ref, c_ref, col_ref, row_ref, bcast_ref, dsk_ref = refs[:7]
    del refs[:7]
    init_ref = refs.pop(0) if has_init else None
    y_ref = refs.pop(0)
    fin_ref = refs.pop(0) if want_final else None
    st_scr, y_scr = refs

    p = SSD_HEAD_DIM
    row = lax.broadcasted_iota(jnp.int32, (CHUNK, CHUNK), 0)
    col = lax.broadcasted_iota(jnp.int32, (CHUNK, CHUNK), 1)
    lo_half = col < p

    if has_init:
        for d in (0, 1):
            for q in range(PAIRS):
                blk = init_ref[d, 2 * q:2 * q + 2].reshape(2 * p, SSD_D_STATE)
                st_scr[d, :, q * LANES:(q + 1) * LANES] = blk.T

    def chunk_body(i, carry, first, zero_state=False):
        for d in (0, 1):
            causal = (row >= col) if d == 0 else (row <= col)
            c = i if d == 0 else nc - 1 - i
            r0 = c * CHUNK if isinstance(c, int) else pl.multiple_of(c * CHUNK, CHUNK)
            rows = pl.ds(r0, CHUNK)
            bc = b_ref[rows, :]
            cc = c_ref[rows, :]
            g = lax.dot_general(cc, bc, (((1,), (1,)), ((), ())), preferred_element_type=F32)
            gm = jnp.where(causal, g, 0.0)
            bt = bc.astype(F32).T
            cols = col_ref[rows, :]
            edge = CHUNK - 1 if d == 0 else 0
            if not zero_state:
                y_off = _dot(cc, st_scr[d].astype(BF16))
            spread = _dot(cols, bcast_ref[d])
            xbd_q, m_q, bt_q, e_q = [], [], [], []
            for q in range(PAIRS):
                x = xs_ref[rows, q * LANES:(q + 1) * LANES]
                zero = jnp.zeros_like(x)
                xbd_q.append(jnp.concatenate([jnp.where(lo_half, x, zero), jnp.where(lo_half, zero, x)], axis=0))
                m_h, bt_h, e_h = [], [], []
                for h in (2 * q, 2 * q + 1):
                    j = d * HEADS_PER_GROUP + h
                    cs_l = spread[:, h * LANES:(h + 1) * LANES]
                    cs_s = row_ref[c, pl.ds(j, 1), :]
                    dt_s = row_ref[c, pl.ds(N_DH + j, 1), :]
                    w_s = row_ref[c, pl.ds(2 * N_DH + j, 1), :]
                    lmat = jnp.exp2(jnp.minimum(cs_l - cs_s, 0.0))
                    m_h.append((gm * lmat * dt_s).astype(BF16))
                    bt_h.append((bt * w_s).astype(BF16))
                    e_h.append(cs_l)
                m_q.append(jnp.concatenate(m_h, axis=1))
                bt_q.append(jnp.concatenate(bt_h, axis=1))
                if not zero_state:
                    e_q.append(jnp.exp2(jnp.where(lo_half, e_h[0], e_h[1])))
            y_diag = [_dot(m_q[q], xbd_q[q]) for q in range(PAIRS)]
            s_new = [_dot(bt_q[q], xbd_q[q]) for q in range(PAIRS)]
            for q in range(PAIRS):
                lanes = slice(q * LANES, (q + 1) * LANES)
                y = y_diag[q] if zero_state else y_diag[q] + y_off[:, lanes] * e_q[q]
                if first:
                    y_scr[rows, lanes] = y + dsk_ref[:, lanes] * xs_ref[rows, lanes].astype(F32)
                else:
                    y_ref[rows, lanes] = (y_scr[rows, lanes] + y).astype(y_ref.dtype)
                if zero_state:
                    st_scr[d, :, lanes] = s_new[q]
                else:
                    st_scr[d, :, lanes] = st_scr[d, :, lanes] * e_q[q][edge:edge + 1, :] + s_new[q]
        return carry

    assert nc % 2 == 0
    start = 0
    if not has_init:
        chunk_body(0, 0, first=True, zero_state=True)
        start = 1
    lax.fori_loop(start, nc // 2, functools.partial(chunk_body, first=True), 0)
    lax.fori_loop(nc // 2, nc, functools.partial(chunk_body, first=False), 0)

    if want_final:
        for d in (0, 1):
            for q in range(PAIRS):
                lanes = slice(q * LANES, (q + 1) * LANES)
                fin_ref[d, 2 * q:2 * q + 2] = st_scr[d, :, lanes].T.reshape(2, p, SSD_D_STATE)


def _ssd_scan(xbc, cols_g, rows_g, d_skip, init, batch, seq, *, want_final):
    nc = seq // CHUNK
    width = SSD_GROUPS * GROUP_W
    heads = SSD_GROUPS * HEADS_PER_GROUP
    has_init = init is not None
    bcast = _lane_bcast_matrices()
    in_specs = [
        pl.BlockSpec((seq, GROUP_W), lambda b, g: (b, g)),
        pl.BlockSpec((seq, SSD_D_STATE), lambda b, g: (b, width // SSD_D_STATE + g)),
        pl.BlockSpec((seq, SSD_D_STATE), lambda b, g: (b, width // SSD_D_STATE + SSD_GROUPS + g)),
        pl.BlockSpec((None, seq, LANES), lambda b, g: (g, b, 0)),
        pl.BlockSpec((None, nc, 3 * N_DH, CHUNK), lambda b, g: (g, b, 0, 0)),
        pl.BlockSpec(bcast.shape, lambda b, g: (0, 0, 0)),
        pl.BlockSpec((1, GROUP_W), lambda b, g: (0, g)),
    ]
    args = [xbc, xbc, xbc, cols_g, rows_g, bcast, d_skip]
    state_spec = pl.BlockSpec((None, 2, HEADS_PER_GROUP, SSD_HEAD_DIM, SSD_D_STATE),
                              lambda b, g: (b, 0, g, 0, 0))
    if has_init:
        in_specs.append(state_spec)
        args.append(init)
    out_specs = [pl.BlockSpec((seq, GROUP_W), lambda b, g: (b, g))]
    out_shape = [jax.ShapeDtypeStruct((batch * seq, width), BF16)]
    if want_final:
        out_specs.append(state_spec)
        out_shape.append(jax.ShapeDtypeStruct((batch, 2, heads, SSD_HEAD_DIM, SSD_D_STATE), F32))
    out = pl.pallas_call(
        functools.partial(_ssd_kernel, nc=nc, has_init=has_init, want_final=want_final),
        grid=(batch, SSD_GROUPS),
        in_specs=in_specs,
        out_specs=out_specs,
        out_shape=out_shape,
        scratch_shapes=[pltpu.VMEM((2, SSD_D_STATE, GROUP_W), F32), pltpu.VMEM((seq, GROUP_W), F32)],
        compiler_params=_params(("parallel", "parallel"), 48),
        name="ssd_scan",
    )(*args)
    return out if want_final else (out[0], None)


def _trunk(x, pos, batch, seq, per_seq_cond, init, mod5, wts, *, want_final):
    (ln_g, ln_b, fno_w_in, fno_w_out, ssd_w_in, conv_w, conv_b, dt_bias, a_log,
     d_skip, norm_w, ssd_w_out) = wts
    fno_width = fno_w_out.shape[0]
    ssd_width = ssd_w_out.shape[0]
    tm_in, tm_out, tm_gated = 1024, 512, 256

    def cond_row(tm):
        if not per_seq_cond:
            return lambda i: CTX_ROW
        assert seq % tm == 0
        return lambda i: (i * tm) // seq

    tn0 = 2048 if pos is None else 1024
    uz = _ln_in_proj(x, pos, mod5, 0, cond_row(tm_in), fno_w_in, False, tm=tm_in, tn=tn0)
    yg = _fno_core(uz, batch, seq, fno_width, tw=min(fno_width, 4096 * 256 // seq))
    x1 = _out_proj_ln(yg, None, None, fno_w_out, x, pos, mod5, 0, cond_row(tm_out), ln_g, ln_b, tm=tm_out)

    zx, dt_raw = _ln_in_proj(x1, None, mod5, 1, cond_row(tm_in), ssd_w_in, True, tm=tm_in, tn=2048)
    conv_tc = 1024 if seq >= 1024 else 2048
    xbc = _conv_silu(zx, conv_w, conv_b, batch, seq, ssd_width, tc=conv_tc)
    cols_g, rows_g = _dt_prep(dt_raw, dt_bias, a_log, n_ch=8)
    y, fin = _ssd_scan(xbc, cols_g, rows_g, d_skip, init, batch, seq, want_final=want_final)
    x2 = _out_proj_ln(y, zx, norm_w, ssd_w_out, x1, None, mod5, 1, cond_row(tm_gated), ln_g, ln_b, tm=tm_gated)
    return x2, fin


def _sincos(pos, dim):
    omega = 1.0 / (10000.0 ** (np.arange(dim // 2, dtype=np.float64) / (dim / 2)))
    ang = pos.astype(np.float64)[:, None] * omega[None, :]
    return np.concatenate([np.sin(ang), np.cos(ang)], axis=-1)


def _grid_pos_embed(n_tokens, dim):
    t = np.arange(n_tokens)
    return np.concatenate([_sincos(t // GRID_W, dim // 2), _sincos(t % GRID_W, dim // 2)], axis=-1)


def kernel(x_prompt, x_sample, state_ssd_ctx, c, c_ctx, w_ada, b_ada, ln_g, ln_b, fno_w_in, fno_w_out,
           ssd_w_in, ssd_conv_w, ssd_conv_b, ssd_dt_bias, ssd_a_log, ssd_d, ssd_norm_w, ssd_w_out):
    batch, seq, d = x_prompt.shape
    dec_batch, dec_seq, _ = x_sample.shape
    ssd_width = ssd_w_out.shape[1]
    conv_dim = ssd_conv_w.shape[2]
    assert dec_batch <= CTX_ROW

    cond = jnp.zeros((COND_ROWS, d), F32).at[:dec_batch].set(c).at[CTX_ROW].set(c_ctx)
    mod = _modulation(cond, w_ada, b_ada)
    mod5 = mod.reshape(DEPTH, COND_ROWS, 3, 1, d)

    assert ssd_w_in.shape[2] == ssd_width + conv_dim + LANES
    wts = (ln_g, ln_b,
           fno_w_in[0].astype(BF16), fno_w_out[0].astype(BF16), ssd_w_in[0].astype(BF16),
           ssd_conv_w[0], ssd_conv_b[0], ssd_dt_bias[0], ssd_a_log[0],
           jnp.repeat(ssd_d[0], SSD_HEAD_DIM).reshape(1, ssd_width),
           ssd_norm_w[0], ssd_w_out[0].astype(BF16))

    y_prompt, fin = _trunk(x_prompt.reshape(batch * seq, d), None, batch, seq,
                           False, None, mod5, wts, want_final=True)
    pos = jnp.asarray(_grid_pos_embed(dec_seq, d), dtype=F32)
    y_sample, _ = _trunk(x_sample.reshape(dec_batch * dec_seq, d), pos, dec_batch, dec_seq,
                         True, state_ssd_ctx[:, 0], mod5, wts, want_final=False)
    return (y_prompt.reshape(batch, seq, d), y_sample.reshape(dec_batch, dec_seq, d),
            fin[:, None])
```

```python
import functools
import math

import numpy as np
import jax
import jax.numpy as jnp
from jax import lax
from jax.experimental import pallas as pl
from jax.experimental.pallas import tpu as pltpu

F32 = jnp.float32
BF16 = jnp.bfloat16

DEPTH = 2
GRID_W = 64
FNO_GROUP_DIM = 256
SSD_HEAD_DIM = 64
SSD_GROUPS = 8
SSD_D_STATE = 128
SSD_CONV_W = 5
CHUNK = 128
DEEPNORM_ALPHA = (2 * DEPTH) ** 0.25
LN_EPS = 1e-5
COND_ROWS = 16
CTX_ROW = 8
LANES = 128
MIB = 1024 * 1024


def _params(semantics, vmem_mib):
    return pltpu.CompilerParams(dimension_semantics=semantics, vmem_limit_bytes=vmem_mib * MIB)


def _silu(x):
    return x * jax.nn.sigmoid(x)


def _dot(a, b):
    return jnp.dot(a, b, preferred_element_type=F32)


def _mod_kernel(c_ref, w_ref, b_ref, o_ref):
    s = _silu(c_ref[...]).astype(BF16)
    o_ref[...] = _dot(s, w_ref[...].astype(BF16)) + b_ref[...]


def _modulation(cond, w_ada, b_ada):
    depth, d, n = w_ada.shape
    tn = 512
    return pl.pallas_call(
        _mod_kernel,
        grid=(depth, n // tn),
        in_specs=[
            pl.BlockSpec((COND_ROWS, d), lambda i, j: (0, 0)),
            pl.BlockSpec((None, d, tn), lambda i, j: (i, 0, j)),
            pl.BlockSpec((None, 1, tn), lambda i, j: (i, 0, j)),
        ],
        out_specs=pl.BlockSpec((None, COND_ROWS, tn), lambda i, j: (i, 0, j)),
        out_shape=jax.ShapeDtypeStruct((depth, COND_ROWS, n), F32),
        compiler_params=_params(("parallel", "parallel"), 32),
        name="adaln_mod",
    )(cond, w_ada, b_ada.reshape(depth, 1, n))


LN_ROWS = 256


def _ln_modulate(x, shift, scale):
    mu = jnp.mean(x, axis=-1, keepdims=True)
    xc = x - mu
    var = jnp.mean(xc * xc, axis=-1, keepdims=True)
    return xc * lax.rsqrt(var + LN_EPS) * (1.0 + scale) + shift


def _ln_mm_kernel(*refs, has_pos, has_dt):
    refs = list(refs)
    x_ref = refs.pop(0)
    pos_ref = refs.pop(0) if has_pos else None
    shift_ref, scale_ref, w_ref = refs.pop(0), refs.pop(0), refs.pop(0)
    wdt_ref = refs.pop(0) if has_dt else None
    o_ref = refs.pop(0)
    dt_ref = refs.pop(0) if has_dt else None
    h_scr = refs.pop(0)

    @pl.when(pl.program_id(1) == 0)
    def _():
        piece = min(LN_ROWS, x_ref.shape[0])
        for r in range(0, x_ref.shape[0], piece):
            rows = slice(r, r + piece)
            x = x_ref[rows, :]
            if has_pos:
                x = x + pos_ref[rows, :]
            h = _ln_modulate(x, shift_ref[...], scale_ref[...]).astype(BF16)
            h_scr[rows, :] = h
            if has_dt:
                dt_ref[rows, :] = _dot(h, wdt_ref[...])

    o_ref[...] = _dot(h_scr[...], w_ref[...]).astype(o_ref.dtype)


def _ln_in_proj(x, pos, mod5, layer, cond_row, w, with_dt, *, tm, tn):
    t, d = x.shape
    has_pos, has_dt = pos is not None, with_dt
    n = w.shape[1] - (LANES if has_dt else 0)
    in_specs = [pl.BlockSpec((tm, d), lambda i, j: (i, 0))]
    args = [x]
    if has_pos:
        pos_blocks = pos.shape[0] // tm
        mode = dict(pipeline_mode=pl.Buffered(1)) if pos_blocks == 1 else {}
        in_specs.append(pl.BlockSpec((tm, d), lambda i, j: (i % pos_blocks, 0), **mode))
        args.append(pos)
    for part in (0, 1):
        in_specs.append(pl.BlockSpec((None, None, None, 1, d),
                                     lambda i, j, part=part: (layer, cond_row(i), part, 0, 0)))
        args.append(mod5)
    in_specs.append(pl.BlockSpec((d, tn), lambda i, j: (0, j)))
    args.append(w)
    out_specs = [pl.BlockSpec((tm, tn), lambda i, j: (i, j))]
    out_shape = [jax.ShapeDtypeStruct((t, n), BF16)]
    if has_dt:
        in_specs.append(pl.BlockSpec((d, LANES), lambda i, j: (0, n // LANES)))
        args.append(w)
        out_specs.append(pl.BlockSpec((tm, LANES), lambda i, j: (i, 0)))
        out_shape.append(jax.ShapeDtypeStruct((t, LANES), F32))
    out = pl.pallas_call(
        functools.partial(_ln_mm_kernel, has_pos=has_pos, has_dt=has_dt),
        grid=(t // tm, n // tn),
        in_specs=in_specs,
        out_specs=out_specs,
        out_shape=out_shape,
        scratch_shapes=[pltpu.VMEM((tm, d), BF16)],
        compiler_params=_params(("parallel", "arbitrary"), 56),
        name="ln_in_proj",
    )(*args)
    return out if has_dt else out[0]


DFT_TAIL = 16


def _dft_matrices(seq, group_dim):
    def cos_sin(n):
        k = np.arange(n, dtype=np.int64)
        ang = 2.0 * np.pi * ((k[:, None] * k[None, :]) % n).astype(np.float64) / n
        return np.cos(ang), np.sin(ang)
    half = seq // 2
    cl, sl = cos_sin(seq)
    cc, sc = cos_sin(group_dim)
    norm = 1.0 / math.sqrt(seq * group_dim)
    posm = np.zeros((2 * half + DFT_TAIL, seq))
    posm[:half] = cl[:half] * norm
    posm[half:2 * half] = sl[:half] * norm
    posm[2 * half] = cl[half] * norm
    flip = np.zeros((half, half))
    flip[0, 0] = 1.0
    flip[np.arange(1, half), half - np.arange(1, half)] = 1.0
    as_bf16 = lambda a: jnp.asarray(a, dtype=F32).astype(BF16)
    return as_bf16(posm), as_bf16(np.stack([cc, sc])), as_bf16(flip)


def _fno_kernel(u_ref, z_ref, posm_ref, chan_ref, flip_ref, o_ref):
    seq, tw = u_ref.shape
    half = seq // 2
    cg = FNO_GROUP_DIM
    first_row = lax.broadcasted_iota(jnp.int32, (half, cg), 0) == 0
    groups = [slice(k * cg, (k + 1) * cg) for k in range(tw // cg)]
    ab = _dot(posm_ref[...], u_ref[...]).astype(BF16)
    p = [_dot(ab[:half, cols], chan_ref[0]) for cols in groups]
    q = [_dot(ab[half:2 * half, cols], chan_ref[1]) for cols in groups]
    p_nyq = [_dot(ab[2 * half:, cols], chan_ref[0])[0:1, :] for cols in groups]
    mirrored = jnp.concatenate(
        [jnp.where(first_row, p_nyq[k], p[k] + q[k]).astype(BF16) for k in range(len(groups))], axis=1)
    bottom = _dot(flip_ref[...], mirrored)
    for k, cols in enumerate(groups):
        gate = _silu(z_ref[:, cols].astype(F32))
        o_ref[:half, cols] = ((p[k] - q[k]) * gate[:half]).astype(o_ref.dtype)
        o_ref[half:, cols] = (bottom[:, cols] * gate[half:]).astype(o_ref.dtype)


def _fno_core(uz, batch, seq, width, *, tw):
    consts = _dft_matrices(seq, FNO_GROUP_DIM)
    nw = width // tw
    const_specs = [pl.BlockSpec(c.shape, lambda b, j, nd=c.ndim: (0,) * nd) for c in consts]
    return pl.pallas_call(
        _fno_kernel,
        grid=(batch, nw),
        in_specs=[
            pl.BlockSpec((seq, tw), lambda b, j: (b, j)),
            pl.BlockSpec((seq, tw), lambda b, j: (b, nw + j)),
        ] + const_specs,
        out_specs=pl.BlockSpec((seq, tw), lambda b, j: (b, j)),
        out_shape=jax.ShapeDtypeStruct((batch * seq, width), BF16),
        compiler_params=_params(("parallel", "parallel"), 48),
        name="fno_core",
    )(uz, uz, *consts)


def _out_ln_kernel(*refs, has_pos, gated):
    refs = list(refs)
    a_ref = refs.pop(0)
    z_ref, nw_ref = (refs.pop(0), refs.pop(0)) if gated else (None, None)
    w_ref, x_ref = refs.pop(0), refs.pop(0)
    pos_ref = refs.pop(0) if has_pos else None
    gate_ref, g_ref, b_ref, o_ref = refs

    if gated:
        y = a_ref[...].astype(F32) * _silu(z_ref[...].astype(F32))
        inv_rms = lax.rsqrt(jnp.mean(y * y, axis=-1, keepdims=True) + LN_EPS)
        acc = _dot((y * nw_ref[...]).astype(BF16), w_ref[...]) * inv_rms
    else:
        acc = _dot(a_ref[...], w_ref[...])
    x = x_ref[...]
    if has_pos:
        x = x + pos_ref[...]
    r = DEEPNORM_ALPHA * x + gate_ref[...] * acc
    mu = jnp.mean(r, axis=-1, keepdims=True)
    rc = r - mu
    var = jnp.mean(rc * rc, axis=-1, keepdims=True)
    o_ref[...] = rc * lax.rsqrt(var + LN_EPS) * g_ref[...] + b_ref[...]


def _out_proj_ln(a, z, norm_w, w, x, pos, mod5, layer, cond_row, ln_g, ln_b, *, tm):
    t, kdim = a.shape
    d = w.shape[1]
    has_pos, gated = pos is not None, z is not None
    row_k = pl.BlockSpec((tm, kdim), lambda i: (i, 0))
    row_d = pl.BlockSpec((tm, d), lambda i: (i, 0))
    in_specs, args = [row_k], [a]
    if gated:
        in_specs += [row_k, pl.BlockSpec((1, kdim), lambda i: (0, 0))]
        args += [z, norm_w.reshape(1, kdim)]
    in_specs += [pl.BlockSpec((kdim, d), lambda i: (0, 0), pipeline_mode=pl.Buffered(1)), row_d]
    args += [w, x]
    if has_pos:
        pos_blocks = pos.shape[0] // tm
        in_specs.append(pl.BlockSpec((tm, d), lambda i: (i % pos_blocks, 0)))
        args.append(pos)
    in_specs.append(pl.BlockSpec((None, None, None, 1, d), lambda i: (layer, cond_row(i), 2, 0, 0)))
    args.append(mod5)
    for v in (ln_g, ln_b):
        in_specs.append(pl.BlockSpec((None, 1, d), lambda i: (layer, 0, 0)))
        args.append(v.reshape(v.shape[0], 1, d))
    return pl.pallas_call(
        functools.partial(_out_ln_kernel, has_pos=has_pos, gated=gated),
        grid=(t // tm,),
        in_specs=in_specs,
        out_specs=row_d,
        out_shape=jax.ShapeDtypeStruct((t, d), F32),
        compiler_params=_params(("parallel",), 56),
        name="out_proj_ln",
    )(*args)


CONV_OFFSETS = (-2, -1, 1, 2)
CONV_PAD = 16
CONV_STRIP = 256


def _shift_matrix():
    s = np.zeros((len(CONV_OFFSETS) * CHUNK, CHUNK + 2 * CONV_PAD), np.float32)
    for t, off in enumerate(CONV_OFFSETS):
        s[t * CHUNK + np.arange(CHUNK), CONV_PAD + np.arange(CHUNK) + off] = 1.0
    return jnp.asarray(s, dtype=BF16)


def _conv_kernel(x_ref, s_ref, w_ref, b_ref, o_ref, pad_scr):
    seq, tc = x_ref.shape
    pad_scr[0:CONV_PAD, :] = jnp.zeros((CONV_PAD, tc), BF16)
    pad_scr[CONV_PAD + seq:2 * CONV_PAD + seq, :] = jnp.zeros((CONV_PAD, tc), BF16)
    pad_scr[CONV_PAD:CONV_PAD + seq, :] = x_ref[...]
    centre = SSD_CONV_W // 2
    for i in range(seq // CHUNK):
        for c in range(0, tc, CONV_STRIP):
            lanes = slice(c, c + CONV_STRIP)
            window = pad_scr[i * CHUNK:(i + 1) * CHUNK + 2 * CONV_PAD, lanes]
            taps = _dot(s_ref[...], window)
            acc = (b_ref[:, lanes]
                   + window[CONV_PAD:CONV_PAD + CHUNK, :].astype(F32) * w_ref[centre:centre + 1, lanes])
            for t, off in enumerate(CONV_OFFSETS):
                acc = acc + taps[t * CHUNK:(t + 1) * CHUNK, :] * w_ref[centre + off:centre + off + 1, lanes]
            o_ref[i * CHUNK:(i + 1) * CHUNK, lanes] = _silu(acc).astype(o_ref.dtype)


def _conv_silu(zx, conv_w, conv_b, batch, seq, col0, *, tc):
    conv_dim = conv_w.shape[1]
    assert col0 % tc == 0 and conv_dim % tc == 0
    c0 = col0 // tc
    w8 = jnp.zeros((8, conv_dim), F32).at[:SSD_CONV_W].set(conv_w)
    shift = _shift_matrix()
    return pl.pallas_call(
        _conv_kernel,
        grid=(batch, conv_dim // tc),
        in_specs=[pl.BlockSpec((seq, tc), lambda b, j: (b, c0 + j)),
                  pl.BlockSpec(shift.shape, lambda b, j: (0, 0)),
                  pl.BlockSpec((8, tc), lambda b, j: (0, j)),
                  pl.BlockSpec((1, tc), lambda b, j: (0, j))],
        out_specs=pl.BlockSpec((seq, tc), lambda b, j: (b, j)),
        out_shape=jax.ShapeDtypeStruct((batch * seq, conv_dim), BF16),
        scratch_shapes=[pltpu.VMEM((seq + 2 * CONV_PAD, tc), BF16)],
        compiler_params=_params(("parallel", "parallel"), 48),
        name="ssd_conv_silu",
    )(zx, shift, w8, conv_b.reshape(1, conv_dim))


def _split3(x):
    hi = x.astype(BF16)
    r = x - hi.astype(F32)
    mid = r.astype(BF16)
    lo = (r - mid.astype(F32)).astype(BF16)
    return hi, mid, lo


HEADS_PER_GROUP = 8
PAIRS = HEADS_PER_GROUP // 2
GROUP_W = HEADS_PER_GROUP * SSD_HEAD_DIM
N_DH = 2 * HEADS_PER_GROUP
SPLIT = 3
LOG2E = 1.4426950408889634


def _group_select_matrix():
    sel = np.zeros((SPLIT * LANES, SSD_GROUPS * LANES), np.float32)
    half = LANES // 2
    for g in range(SSD_GROUPS):
        for j in range(N_DH):
            d, h = divmod(j, HEADS_PER_GROUP)
            src = d * half + g * HEADS_PER_GROUP + h
            for p in range(SPLIT):
                sel[p * LANES + src, g * LANES + SPLIT * j + p] = 1.0
    return jnp.asarray(sel, dtype=BF16)


def _lane_bcast_matrices():
    e = np.zeros((2, LANES, HEADS_PER_GROUP * LANES), np.float32)
    for d in range(2):
        for h in range(HEADS_PER_GROUP):
            j = d * HEADS_PER_GROUP + h
            e[d, SPLIT * j:SPLIT * (j + 1), h * LANES:(h + 1) * LANES] = 1.0
    return jnp.asarray(e, dtype=BF16)


def _dt_kernel(raw_ref, bias_ref, alog_ref, sel_ref, col_ref, row_ref):
    n_ch = row_ref.shape[1]
    half = LANES // 2
    row = lax.broadcasted_iota(jnp.int32, (CHUNK, CHUNK), 0)
    col = lax.broadcasted_iota(jnp.int32, (CHUNK, CHUNK), 1)
    tri_f = jnp.where(row >= col, 1.0, 0.0).astype(BF16)
    tri_b = jnp.where(row <= col, 1.0, 0.0).astype(BF16)
    is_fwd = col < half
    a2 = -jnp.exp(alog_ref[...]) * LOG2E
    for k in range(n_ch):
        rows = slice(k * CHUNK, (k + 1) * CHUNK)
        x = raw_ref[rows, :] + bias_ref[...]
        dt = jnp.maximum(x, 0.0) + jnp.log1p(jnp.exp(-jnp.abs(x)))
        parts = _split3(dt * a2)
        cs = jnp.where(is_fwd, sum(_dot(tri_f, p) for p in parts), sum(_dot(tri_b, p) for p in parts))
        tot = jnp.where(is_fwd[:1], cs[CHUNK - 1:CHUNK, :], cs[0:1, :])
        w = dt * jnp.exp2(tot - cs)
        pieces = jnp.concatenate(_split3(cs), axis=1)
        grouped = _dot(pieces, sel_ref[...]).astype(BF16)
        transposed = (cs.T, dt.T, w.T)
        for g in range(SSD_GROUPS):
            col_ref[g, rows, :] = grouped[:, g * LANES:(g + 1) * LANES]
            for n, v in enumerate(transposed):
                for d in (0, 1):
                    src = d * half + g * HEADS_PER_GROUP
                    dst = n * N_DH + d * HEADS_PER_GROUP
                    row_ref[g, k, dst:dst + HEADS_PER_GROUP, :] = v[src:src + HEADS_PER_GROUP, :]


def _dt_prep(dt_raw, dt_bias, a_log, *, n_ch):
    t = dt_raw.shape[0]
    tm = n_ch * CHUNK
    sel = _group_select_matrix()
    vec_spec = pl.BlockSpec((1, LANES), lambda i: (0, 0))
    return pl.pallas_call(
        _dt_kernel,
        grid=(t // tm,),
        in_specs=[pl.BlockSpec((tm, LANES), lambda i: (i, 0)), vec_spec, vec_spec,
                  pl.BlockSpec(sel.shape, lambda i: (0, 0))],
        out_specs=[pl.BlockSpec((SSD_GROUPS, tm, LANES), lambda i: (0, i, 0)),
                   pl.BlockSpec((SSD_GROUPS, n_ch, 3 * N_DH, CHUNK), lambda i: (0, i, 0, 0))],
        out_shape=[jax.ShapeDtypeStruct((SSD_GROUPS, t, LANES), BF16),
                   jax.ShapeDtypeStruct((SSD_GROUPS, t // CHUNK, 3 * N_DH, CHUNK), F32)],
        compiler_params=_params(("parallel",), 32),
        name="ssd_dt_prep",
    )(dt_raw, dt_bias.reshape(1, LANES), a_log.reshape(1, LANES), sel)


SSD_SEQS_PER_STEP = 2


def _ssd_kernel(*refs, nc, nb, has_init, want_final):
    refs = list(refs)
    xs_ref, b_ref, c_ref, col_ref, row_ref, bcast_ref, dsk_ref = refs[:7]
    del refs[:7]
    init_ref = refs.pop(0) if has_init else None
    y_ref = refs.pop(0)
    fin_ref = refs.pop(0) if want_final else None
    st_scr, y_scr = refs

    p = SSD_HEAD_DIM
    row = lax.broadcasted_iota(jnp.int32, (CHUNK, CHUNK), 0)
    col = lax.broadcasted_iota(jnp.int32, (CHUNK, CHUNK), 1)
    lo_half = col < p

    if has_init:
        for sb in range(nb):
            for d in (0, 1):
                for q in range(PAIRS):
                    blk = init_ref[sb, d, 2 * q:2 * q + 2].reshape(2 * p, SSD_D_STATE)
                    st_scr[sb, d, :, q * LANES:(q + 1) * LANES] = blk.T

    def chunk_body(i, carry, first, zero_state=False):
        for sb, d in [(sb, d) for sb in range(nb) for d in (0, 1)]:
            causal = (row >= col) if d == 0 else (row <= col)
            c = sb * nc + (i if d == 0 else nc - 1 - i)
            r0 = c * CHUNK if isinstance(c, int) else pl.multiple_of(c * CHUNK, CHUNK)
            rows = pl.ds(r0, CHUNK)
            bc = b_ref[rows, :]
            cc = c_ref[rows, :]
            g = lax.dot_general(cc, bc, (((1,), (1,)), ((), ())), preferred_element_type=F32)
            gm = jnp.where(causal, g, 0.0)
            bt = bc.astype(F32).T
            cols = col_ref[rows, :]
            edge = CHUNK - 1 if d == 0 else 0
            if not zero_state:
                y_off = _dot(cc, st_scr[sb, d].astype(BF16))
            spread = _dot(cols, bcast_ref[d])
            xbd_q, m_q, bt_q, e_q = [], [], [], []
            for q in range(PAIRS):
                x = xs_ref[rows, q * LANES:(q + 1) * LANES]
                zero = jnp.zeros_like(x)
                xbd_q.append(jnp.concatenate([jnp.where(lo_half, x, zero), jnp.where(lo_half, zero, x)], axis=0))
                m_h, bt_h, e_h = [], [], []
                for h in (2 * q, 2 * q + 1):
                    j = d * HEADS_PER_GROUP + h
                    cs_l = spread[:, h * LANES:(h + 1) * LANES]
                    cs_s = row_ref[c, pl.ds(j, 1), :]
                    dt_s = row_ref[c, pl.ds(N_DH + j, 1), :]
                    w_s = row_ref[c, pl.ds(2 * N_DH + j, 1), :]
                    lmat = jnp.exp2(jnp.minimum(cs_l - cs_s, 0.0))
                    m_h.append((gm * lmat * dt_s).astype(BF16))
                    bt_h.append((bt * w_s).astype(BF16))
                    e_h.append(cs_l)
                m_q.append(jnp.concatenate(m_h, axis=1))
                bt_q.append(jnp.concatenate(bt_h, axis=1))
                if not zero_state:
                    e_q.append(jnp.exp2(jnp.where(lo_half, e_h[0], e_h[1])))
            y_diag = [_dot(m_q[q], xbd_q[q]) for q in range(PAIRS)]
            s_new = [_dot(bt_q[q], xbd_q[q]) for q in range(PAIRS)]
            for q in range(PAIRS):
                lanes = slice(q * LANES, (q + 1) * LANES)
                y = y_diag[q] if zero_state else y_diag[q] + y_off[:, lanes] * e_q[q]
                if first:
                    y_scr[rows, lanes] = y + dsk_ref[:, lanes] * xs_ref[rows, lanes].astype(F32)
                else:
                    y_ref[rows, lanes] = (y_scr[rows, lanes] + y).astype(y_ref.dtype)
                if zero_state:
                    st_scr[sb, d, :, lanes] = s_new[q]
                else:
                    st_scr[sb, d, :, lanes] = (st_scr[sb, d, :, lanes] * e_q[q][edge:edge + 1, :]
                                               + s_new[q])
        return carry

    assert nc % 2 == 0
    start = 0
    if not has_init:
        chunk_body(0, 0, first=True, zero_state=True)
        start = 1
    lax.fori_loop(start, nc // 2, functools.partial(chunk_body, first=True), 0)
    lax.fori_loop(nc // 2, nc, functools.partial(chunk_body, first=False), 0)

    if want_final:
        for sb in range(nb):
            for d in (0, 1):
                for q in range(PAIRS):
                    lanes = slice(q * LANES, (q + 1) * LANES)
                    fin_ref[sb, d, 2 * q:2 * q + 2] = st_scr[sb, d, :, lanes].T.reshape(2, p, SSD_D_STATE)


def _ssd_scan(xbc, cols_g, rows_g, d_skip, init, batch, seq, *, want_final):
    nc = seq // CHUNK
    width = SSD_GROUPS * GROUP_W
    heads = SSD_GROUPS * HEADS_PER_GROUP
    has_init = init is not None
    bcast = _lane_bcast_matrices()
    nb = SSD_SEQS_PER_STEP
    assert batch % nb == 0
    rows = nb * seq
    in_specs = [
        pl.BlockSpec((rows, GROUP_W), lambda b, g: (b, g)),
        pl.BlockSpec((rows, SSD_D_STATE), lambda b, g: (b, width // SSD_D_STATE + g)),
        pl.BlockSpec((rows, SSD_D_STATE), lambda b, g: (b, width // SSD_D_STATE + SSD_GROUPS + g)),
        pl.BlockSpec((None, rows, LANES), lambda b, g: (g, b, 0)),
        pl.BlockSpec((None, nb * nc, 3 * N_DH, CHUNK), lambda b, g: (g, b, 0, 0)),
        pl.BlockSpec(bcast.shape, lambda b, g: (0, 0, 0)),
        pl.BlockSpec((1, GROUP_W), lambda b, g: (0, g)),
    ]
    args = [xbc, xbc, xbc, cols_g, rows_g, bcast, d_skip]
    state_spec = pl.BlockSpec((nb, 2, HEADS_PER_GROUP, SSD_HEAD_DIM, SSD_D_STATE),
                              lambda b, g: (b, 0, g, 0, 0))
    if has_init:
        in_specs.append(state_spec)
        args.append(init)
    out_specs = [pl.BlockSpec((rows, GROUP_W), lambda b, g: (b, g))]
    out_shape = [jax.ShapeDtypeStruct((batch * seq, width), BF16)]
    if want_final:
        out_specs.append(state_spec)
        out_shape.append(jax.ShapeDtypeStruct((batch, 2, heads, SSD_HEAD_DIM, SSD_D_STATE), F32))
    out = pl.pallas_call(
        functools.partial(_ssd_kernel, nc=nc, nb=nb, has_init=has_init, want_final=want_final),
        grid=(batch // nb, SSD_GROUPS),
        in_specs=in_specs,
        out_specs=out_specs,
        out_shape=out_shape,
        scratch_shapes=[pltpu.VMEM((nb, 2, SSD_D_STATE, GROUP_W), F32), pltpu.VMEM((rows, GROUP_W), F32)],
        compiler_params=_params(("parallel", "parallel"), 48),
        name="ssd_scan",
    )(*args)
    return out if want_final else (out[0], None)


def _trunk(x, pos, batch, seq, per_seq_cond, init, mod5, wts, *, want_final):
    (ln_g, ln_b, fno_w_in, fno_w_out, ssd_w_in, conv_w, conv_b, dt_bias, a_log,
     d_skip, norm_w, ssd_w_out) = wts
    fno_width = fno_w_out.shape[0]
    ssd_width = ssd_w_out.shape[0]
    tm_in, tm_out, tm_gated = 1024, 512, 256

    def cond_row(tm):
        if not per_seq_cond:
            return lambda i: CTX_ROW
        assert seq % tm == 0
        return lambda i: (i * tm) // seq

    tn0 = 2048 if pos is None else 1024
    uz = _ln_in_proj(x, pos, mod5, 0, cond_row(tm_in), fno_w_in, False, tm=tm_in, tn=tn0)
    yg = _fno_core(uz, batch, seq, fno_width, tw=min(fno_width, 4096 * 256 // seq))
    x1 = _out_proj_ln(yg, None, None, fno_w_out, x, pos, mod5, 0, cond_row(tm_out), ln_g, ln_b, tm=tm_out)

    zx, dt_raw = _ln_in_proj(x1, None, mod5, 1, cond_row(tm_in), ssd_w_in, True, tm=tm_in, tn=2048)
    xbc = _conv_silu(zx, conv_w, conv_b, batch, seq, ssd_width, tc=2048)
    cols_g, rows_g = _dt_prep(dt_raw, dt_bias, a_log, n_ch=8)
    y, fin = _ssd_scan(xbc, cols_g, rows_g, d_skip, init, batch, seq, want_final=want_final)
    x2 = _out_proj_ln(y, zx, norm_w, ssd_w_out, x1, None, mod5, 1, cond_row(tm_gated), ln_g, ln_b, tm=tm_gated)
    return x2, fin


def _sincos(pos, dim):
    omega = 1.0 / (10000.0 ** (np.arange(dim // 2, dtype=np.float64) / (dim / 2)))
    ang = pos.astype(np.float64)[:, None] * omega[None, :]
    return np.concatenate([np.sin(ang), np.cos(ang)], axis=-1)


def _grid_pos_embed(n_tokens, dim):
    t = np.arange(n_tokens)
    return np.concatenate([_sincos(t // GRID_W, dim // 2), _sincos(t % GRID_W, dim // 2)], axis=-1)


def kernel(x_prompt, x_sample, state_ssd_ctx, c, c_ctx, w_ada, b_ada, ln_g, ln_b, fno_w_in, fno_w_out,
           ssd_w_in, ssd_conv_w, ssd_conv_b, ssd_dt_bias, ssd_a_log, ssd_d, ssd_norm_w, ssd_w_out):
    batch, seq, d = x_prompt.shape
    dec_batch, dec_seq, _ = x_sample.shape
    ssd_width = ssd_w_out.shape[1]
    conv_dim = ssd_conv_w.shape[2]
    assert dec_batch <= CTX_ROW

    cond = jnp.zeros((COND_ROWS, d), F32).at[:dec_batch].set(c).at[CTX_ROW].set(c_ctx)
    mod = _modulation(cond, w_ada, b_ada)
    mod5 = mod.reshape(DEPTH, COND_ROWS, 3, 1, d)

    assert ssd_w_in.shape[2] == ssd_width + conv_dim + LANES
    wts = (ln_g, ln_b,
           fno_w_in[0].astype(BF16), fno_w_out[0].astype(BF16), ssd_w_in[0].astype(BF16),
           ssd_conv_w[0], ssd_conv_b[0], ssd_dt_bias[0], ssd_a_log[0],
           jnp.repeat(ssd_d[0], SSD_HEAD_DIM).reshape(1, ssd_width),
           ssd_norm_w[0], ssd_w_out[0].astype(BF16))

    y_prompt, fin = _trunk(x_prompt.reshape(batch * seq, d), None, batch, seq,
                           False, None, mod5, wts, want_final=True)
    pos = jnp.asarray(_grid_pos_embed(dec_seq, d), dtype=F32)
    y_sample, _ = _trunk(x_sample.reshape(dec_batch * dec_seq, d), pos, dec_batch, dec_seq,
                         True, state_ssd_ctx[:, 0], mod5, wts, want_final=False)
    return (y_prompt.reshape(batch, seq, d), y_sample.reshape(dec_batch, dec_seq, d),
            fin[:, None])
```

```python
import functools
import math

import numpy as np
import jax
import jax.numpy as jnp
from jax import lax
from jax.experimental import pallas as pl
from jax.experimental.pallas import tpu as pltpu

F32 = jnp.float32
BF16 = jnp.bfloat16

DEPTH = 2
GRID_W = 64
FNO_GROUP_DIM = 256
SSD_HEAD_DIM = 64
SSD_GROUPS = 8
SSD_D_STATE = 128
SSD_CONV_W = 5
CHUNK = 128
DEEPNORM_ALPHA = (2 * DEPTH) ** 0.25
LN_EPS = 1e-5
COND_ROWS = 16
CTX_ROW = 8
LANES = 128
MIB = 1024 * 1024


def _params(semantics, vmem_mib):
    return pltpu.CompilerParams(dimension_semantics=semantics, vmem_limit_bytes=vmem_mib * MIB)


def _silu(x):
    return x * jax.nn.sigmoid(x)


def _dot(a, b):
    return jnp.dot(a, b, preferred_element_type=F32)


def _mod_kernel(c_ref, w_ref, b_ref, o_ref):
    s = _silu(c_ref[...]).astype(BF16)
    o_ref[...] = _dot(s, w_ref[...].astype(BF16)) + b_ref[...]


def _modulation(cond, w_ada, b_ada):
    depth, d, n = w_ada.shape
    tn = 512
    return pl.pallas_call(
        _mod_kernel,
        grid=(depth, n // tn),
        in_specs=[
            pl.BlockSpec((COND_ROWS, d), lambda i, j: (0, 0)),
            pl.BlockSpec((None, d, tn), lambda i, j: (i, 0, j)),
            pl.BlockSpec((None, 1, tn), lambda i, j: (i, 0, j)),
        ],
        out_specs=pl.BlockSpec((None, COND_ROWS, tn), lambda i, j: (i, 0, j)),
        out_shape=jax.ShapeDtypeStruct((depth, COND_ROWS, n), F32),
        compiler_params=_params(("parallel", "parallel"), 32),
        name="adaln_mod",
    )(cond, w_ada, b_ada.reshape(depth, 1, n))


LN_ROWS = 256


def _ln_modulate(x, shift, scale):
    mu = jnp.mean(x, axis=-1, keepdims=True)
    xc = x - mu
    var = jnp.mean(xc * xc, axis=-1, keepdims=True)
    return xc * lax.rsqrt(var + LN_EPS) * (1.0 + scale) + shift


def _ln_mm_kernel(*refs, has_pos, has_dt):
    refs = list(refs)
    x_ref = refs.pop(0)
    pos_ref = refs.pop(0) if has_pos else None
    shift_ref, scale_ref, w_ref = refs.pop(0), refs.pop(0), refs.pop(0)
    wdt_ref = refs.pop(0) if has_dt else None
    o_ref = refs.pop(0)
    dt_ref = refs.pop(0) if has_dt else None
    h_scr = refs.pop(0)

    @pl.when(pl.program_id(1) == 0)
    def _():
        piece = min(LN_ROWS, x_ref.shape[0])
        for r in range(0, x_ref.shape[0], piece):
            rows = slice(r, r + piece)
            x = x_ref[rows, :]
            if has_pos:
                x = x + pos_ref[rows, :]
            h = _ln_modulate(x, shift_ref[...], scale_ref[...]).astype(BF16)
            h_scr[rows, :] = h
            if has_dt:
                dt_ref[rows, :] = _dot(h, wdt_ref[...])

    o_ref[...] = _dot(h_scr[...], w_ref[...]).astype(o_ref.dtype)


def _ln_in_proj(x, pos, mod5, layer, cond_row, w, with_dt, *, tm, tn):
    t, d = x.shape
    has_pos, has_dt = pos is not None, with_dt
    n = w.shape[1] - (LANES if has_dt else 0)
    in_specs = [pl.BlockSpec((tm, d), lambda i, j: (i, 0))]
    args = [x]
    if has_pos:
        pos_blocks = pos.shape[0] // tm
        mode = dict(pipeline_mode=pl.Buffered(1)) if pos_blocks == 1 else {}
        in_specs.append(pl.BlockSpec((tm, d), lambda i, j: (i % pos_blocks, 0), **mode))
        args.append(pos)
    for part in (0, 1):
        in_specs.append(pl.BlockSpec((None, None, None, 1, d),
                                     lambda i, j, part=part: (layer, cond_row(i), part, 0, 0)))
        args.append(mod5)
    in_specs.append(pl.BlockSpec((d, tn), lambda i, j: (0, j)))
    args.append(w)
    out_specs = [pl.BlockSpec((tm, tn), lambda i, j: (i, j))]
    out_shape = [jax.ShapeDtypeStruct((t, n), BF16)]
    if has_dt:
        in_specs.append(pl.BlockSpec((d, LANES), lambda i, j: (0, n // LANES)))
        args.append(w)
        out_specs.append(pl.BlockSpec((tm, LANES), lambda i, j: (i, 0)))
        out_shape.append(jax.ShapeDtypeStruct((t, LANES), F32))
    out = pl.pallas_call(
        functools.partial(_ln_mm_kernel, has_pos=has_pos, has_dt=has_dt),
        grid=(t // tm, n // tn),
        in_specs=in_specs,
        out_specs=out_specs,
        out_shape=out_shape,
        scratch_shapes=[pltpu.VMEM((tm, d), BF16)],
        compiler_params=_params(("parallel", "arbitrary"), 56),
        name="ln_in_proj",
    )(*args)
    return out if has_dt else out[0]


DFT_TAIL = 16


def _dft_matrices(seq, group_dim):
    def cos_sin(n):
        k = np.arange(n, dtype=np.int64)
        ang = 2.0 * np.pi * ((k[:, None] * k[None, :]) % n).astype(np.float64) / n
        return np.cos(ang), np.sin(ang)
    half = seq // 2
    cl, sl = cos_sin(seq)
    cc, sc = cos_sin(group_dim)
    norm = 1.0 / math.sqrt(seq * group_dim)
    posm = np.zeros((2 * half + DFT_TAIL, seq))
    posm[:half] = cl[:half] * norm
    posm[half:2 * half] = sl[:half] * norm
    posm[2 * half] = cl[half] * norm
    flip = np.zeros((half, half))
    flip[0, 0] = 1.0
    flip[np.arange(1, half), half - np.arange(1, half)] = 1.0
    as_bf16 = lambda a: jnp.asarray(a, dtype=F32).astype(BF16)
    return as_bf16(posm), as_bf16(np.stack([cc, sc])), as_bf16(flip)


def _fno_kernel(u_ref, z_ref, posm_ref, chan_ref, flip_ref, o_ref):
    seq, tw = u_ref.shape
    half = seq // 2
    cg = FNO_GROUP_DIM
    first_row = lax.broadcasted_iota(jnp.int32, (half, cg), 0) == 0
    groups = [slice(k * cg, (k + 1) * cg) for k in range(tw // cg)]
    ab = _dot(posm_ref[...], u_ref[...]).astype(BF16)
    p = [_dot(ab[:half, cols], chan_ref[0]) for cols in groups]
    q = [_dot(ab[half:2 * half, cols], chan_ref[1]) for cols in groups]
    p_nyq = [_dot(ab[2 * half:, cols], chan_ref[0])[0:1, :] for cols in groups]
    mirrored = jnp.concatenate(
        [jnp.where(first_row, p_nyq[k], p[k] + q[k]).astype(BF16) for k in range(len(groups))], axis=1)
    bottom = _dot(flip_ref[...], mirrored)
    for k, cols in enumerate(groups):
        gate = _silu(z_ref[:, cols].astype(F32))
        o_ref[:half, cols] = ((p[k] - q[k]) * gate[:half]).astype(o_ref.dtype)
        o_ref[half:, cols] = (bottom[:, cols] * gate[half:]).astype(o_ref.dtype)


def _fno_core(uz, batch, seq, width, *, tw):
    consts = _dft_matrices(seq, FNO_GROUP_DIM)
    nw = width // tw
    const_specs = [pl.BlockSpec(c.shape, lambda b, j, nd=c.ndim: (0,) * nd) for c in consts]
    return pl.pallas_call(
        _fno_kernel,
        grid=(batch, nw),
        in_specs=[
            pl.BlockSpec((seq, tw), lambda b, j: (b, j)),
            pl.BlockSpec((seq, tw), lambda b, j: (b, nw + j)),
        ] + const_specs,
        out_specs=pl.BlockSpec((seq, tw), lambda b, j: (b, j)),
        out_shape=jax.ShapeDtypeStruct((batch * seq, width), BF16),
        compiler_params=_params(("parallel", "parallel"), 48),
        name="fno_core",
    )(uz, uz, *consts)


def _out_ln_kernel(*refs, has_pos, gated):
    refs = list(refs)
    a_ref = refs.pop(0)
    z_ref, nw_ref = (refs.pop(0), refs.pop(0)) if gated else (None, None)
    w_ref, x_ref = refs.pop(0), refs.pop(0)
    pos_ref = refs.pop(0) if has_pos else None
    gate_ref, g_ref, b_ref, o_ref = refs

    if gated:
        y = a_ref[...].astype(F32) * _silu(z_ref[...].astype(F32))
        inv_rms = lax.rsqrt(jnp.mean(y * y, axis=-1, keepdims=True) + LN_EPS)
        acc = _dot((y * nw_ref[...]).astype(BF16), w_ref[...]) * inv_rms
    else:
        acc = _dot(a_ref[...], w_ref[...])
    x = x_ref[...]
    if has_pos:
        x = x + pos_ref[...]
    r = DEEPNORM_ALPHA * x + gate_ref[...] * acc
    mu = jnp.mean(r, axis=-1, keepdims=True)
    rc = r - mu
    var = jnp.mean(rc * rc, axis=-1, keepdims=True)
    o_ref[...] = rc * lax.rsqrt(var + LN_EPS) * g_ref[...] + b_ref[...]


def _out_proj_ln(a, z, norm_w, w, x, pos, mod5, layer, cond_row, ln_g, ln_b, *, tm):
    t, kdim = a.shape
    d = w.shape[1]
    has_pos, gated = pos is not None, z is not None
    row_k = pl.BlockSpec((tm, kdim), lambda i: (i, 0))
    row_d = pl.BlockSpec((tm, d), lambda i: (i, 0))
    in_specs, args = [row_k], [a]
    if gated:
        in_specs += [row_k, pl.BlockSpec((1, kdim), lambda i: (0, 0))]
        args += [z, norm_w.reshape(1, kdim)]
    in_specs += [pl.BlockSpec((kdim, d), lambda i: (0, 0), pipeline_mode=pl.Buffered(1)), row_d]
    args += [w, x]
    if has_pos:
        pos_blocks = pos.shape[0] // tm
        in_specs.append(pl.BlockSpec((tm, d), lambda i: (i % pos_blocks, 0)))
        args.append(pos)
    in_specs.append(pl.BlockSpec((None, None, None, 1, d), lambda i: (layer, cond_row(i), 2, 0, 0)))
    args.append(mod5)
    for v in (ln_g, ln_b):
        in_specs.append(pl.BlockSpec((None, 1, d), lambda i: (layer, 0, 0)))
        args.append(v.reshape(v.shape[0], 1, d))
    return pl.pallas_call(
        functools.partial(_out_ln_kernel, has_pos=has_pos, gated=gated),
        grid=(t // tm,),
        in_specs=in_specs,
        out_specs=row_d,
        out_shape=jax.ShapeDtypeStruct((t, d), F32),
        compiler_params=_params(("parallel",), 56),
        name="out_proj_ln",
    )(*args)


CONV_OFFSETS = (-2, -1, 1, 2)
CONV_PAD = 16
CONV_STRIP = 256


def _shift_matrix():
    s = np.zeros((len(CONV_OFFSETS) * CHUNK, CHUNK + 2 * CONV_PAD), np.float32)
    for t, off in enumerate(CONV_OFFSETS):
        s[t * CHUNK + np.arange(CHUNK), CONV_PAD + np.arange(CHUNK) + off] = 1.0
    return jnp.asarray(s, dtype=BF16)


def _conv_kernel(x_ref, s_ref, w_ref, b_ref, o_ref, pad_scr):
    seq, tc = x_ref.shape
    pad_scr[0:CONV_PAD, :] = jnp.zeros((CONV_PAD, tc), BF16)
    pad_scr[CONV_PAD + seq:2 * CONV_PAD + seq, :] = jnp.zeros((CONV_PAD, tc), BF16)
    pad_scr[CONV_PAD:CONV_PAD + seq, :] = x_ref[...]
    centre = SSD_CONV_W // 2
    for i in range(seq // CHUNK):
        for c in range(0, tc, CONV_STRIP):
            lanes = slice(c, c + CONV_STRIP)
            window = pad_scr[i * CHUNK:(i + 1) * CHUNK + 2 * CONV_PAD, lanes]
            taps = _dot(s_ref[...], window)
            acc = (b_ref[:, lanes]
                   + window[CONV_PAD:CONV_PAD + CHUNK, :].astype(F32) * w_ref[centre:centre + 1, lanes])
            for t, off in enumerate(CONV_OFFSETS):
                acc = acc + taps[t * CHUNK:(t + 1) * CHUNK, :] * w_ref[centre + off:centre + off + 1, lanes]
            o_ref[i * CHUNK:(i + 1) * CHUNK, lanes] = _silu(acc).astype(o_ref.dtype)


def _conv_silu(zx, conv_w, conv_b, batch, seq, col0, *, tc):
    conv_dim = conv_w.shape[1]
    assert col0 % tc == 0 and conv_dim % tc == 0
    c0 = col0 // tc
    w8 = jnp.zeros((8, conv_dim), F32).at[:SSD_CONV_W].set(conv_w)
    shift = _shift_matrix()
    return pl.pallas_call(
        _conv_kernel,
        grid=(batch, conv_dim // tc),
        in_specs=[pl.BlockSpec((seq, tc), lambda b, j: (b, c0 + j)),
                  pl.BlockSpec(shift.shape, lambda b, j: (0, 0)),
                  pl.BlockSpec((8, tc), lambda b, j: (0, j)),
                  pl.BlockSpec((1, tc), lambda b, j: (0, j))],
        out_specs=pl.BlockSpec((seq, tc), lambda b, j: (b, j)),
        out_shape=jax.ShapeDtypeStruct((batch * seq, conv_dim), BF16),
        scratch_shapes=[pltpu.VMEM((seq + 2 * CONV_PAD, tc), BF16)],
        compiler_params=_params(("parallel", "parallel"), 48),
        name="ssd_conv_silu",
    )(zx, shift, w8, conv_b.reshape(1, conv_dim))


def _split3(x):
    hi = x.astype(BF16)
    r = x - hi.astype(F32)
    mid = r.astype(BF16)
    lo = (r - mid.astype(F32)).astype(BF16)
    return hi, mid, lo


HEADS_PER_GROUP = 8
PAIRS = HEADS_PER_GROUP // 2
GROUP_W = HEADS_PER_GROUP * SSD_HEAD_DIM
N_DH = 2 * HEADS_PER_GROUP
SPLIT = 3
LOG2E = 1.4426950408889634


def _group_select_matrix():
    sel = np.zeros((SPLIT * LANES, SSD_GROUPS * LANES), np.float32)
    half = LANES // 2
    for g in range(SSD_GROUPS):
        for j in range(N_DH):
            d, h = divmod(j, HEADS_PER_GROUP)
            src = d * half + g * HEADS_PER_GROUP + h
            for p in range(SPLIT):
                sel[p * LANES + src, g * LANES + SPLIT * j + p] = 1.0
    return jnp.asarray(sel, dtype=BF16)


def _lane_bcast_matrices():
    e = np.zeros((2, LANES, HEADS_PER_GROUP * LANES), np.float32)
    for d in range(2):
        for h in range(HEADS_PER_GROUP):
            j = d * HEADS_PER_GROUP + h
            e[d, SPLIT * j:SPLIT * (j + 1), h * LANES:(h + 1) * LANES] = 1.0
    return jnp.asarray(e, dtype=BF16)


def _dt_kernel(raw_ref, bias_ref, alog_ref, sel_ref, col_ref, row_ref):
    n_ch = row_ref.shape[1]
    half = LANES // 2
    row = lax.broadcasted_iota(jnp.int32, (CHUNK, CHUNK), 0)
    col = lax.broadcasted_iota(jnp.int32, (CHUNK, CHUNK), 1)
    tri_f = jnp.where(row >= col, 1.0, 0.0).astype(BF16)
    tri_b = jnp.where(row <= col, 1.0, 0.0).astype(BF16)
    is_fwd = col < half
    a2 = -jnp.exp(alog_ref[...]) * LOG2E
    for k in range(n_ch):
        rows = slice(k * CHUNK, (k + 1) * CHUNK)
        x = raw_ref[rows, :] + bias_ref[...]
        dt = jnp.maximum(x, 0.0) + jnp.log1p(jnp.exp(-jnp.abs(x)))
        parts = _split3(dt * a2)
        cs = jnp.where(is_fwd, sum(_dot(tri_f, p) for p in parts), sum(_dot(tri_b, p) for p in parts))
        tot = jnp.where(is_fwd[:1], cs[CHUNK - 1:CHUNK, :], cs[0:1, :])
        w = dt * jnp.exp2(tot - cs)
        pieces = jnp.concatenate(_split3(cs), axis=1)
        grouped = _dot(pieces, sel_ref[...]).astype(BF16)
        log_dt = jnp.log2(dt)
        transposed = ((cs - log_dt).T, log_dt.T, w.T)
        for g in range(SSD_GROUPS):
            col_ref[g, rows, :] = grouped[:, g * LANES:(g + 1) * LANES]
            for n, v in enumerate(transposed):
                for d in (0, 1):
                    src = d * half + g * HEADS_PER_GROUP
                    dst = n * N_DH + d * HEADS_PER_GROUP
                    row_ref[g, k, dst:dst + HEADS_PER_GROUP, :] = v[src:src + HEADS_PER_GROUP, :]


def _dt_prep(dt_raw, dt_bias, a_log, *, n_ch):
    t = dt_raw.shape[0]
    tm = n_ch * CHUNK
    sel = _group_select_matrix()
    vec_spec = pl.BlockSpec((1, LANES), lambda i: (0, 0))
    return pl.pallas_call(
        _dt_kernel,
        grid=(t // tm,),
        in_specs=[pl.BlockSpec((tm, LANES), lambda i: (i, 0)), vec_spec, vec_spec,
                  pl.BlockSpec(sel.shape, lambda i: (0, 0))],
        out_specs=[pl.BlockSpec((SSD_GROUPS, tm, LANES), lambda i: (0, i, 0)),
                   pl.BlockSpec((SSD_GROUPS, n_ch, 3 * N_DH, CHUNK), lambda i: (0, i, 0, 0))],
        out_shape=[jax.ShapeDtypeStruct((SSD_GROUPS, t, LANES), BF16),
                   jax.ShapeDtypeStruct((SSD_GROUPS, t // CHUNK, 3 * N_DH, CHUNK), F32)],
        compiler_params=_params(("parallel",), 32),
        name="ssd_dt_prep",
    )(dt_raw, dt_bias.reshape(1, LANES), a_log.reshape(1, LANES), sel)


SSD_SEQS_PER_STEP = 2


def _ssd_kernel(*refs, nc, nb, has_init, want_final):
    refs = list(refs)
    xs_ref, b_ref, c_ref, col_ref, row_ref, bcast_ref, dsk_ref = refs[:7]
    del refs[:7]
    init_ref = refs.pop(0) if has_init else None
    y_ref = refs.pop(0)
    fin_ref = refs.pop(0) if want_final else None
    st_scr, y_scr = refs

    p = SSD_HEAD_DIM
    row = lax.broadcasted_iota(jnp.int32, (CHUNK, CHUNK), 0)
    col = lax.broadcasted_iota(jnp.int32, (CHUNK, CHUNK), 1)
    lo_half = col < p

    if has_init:
        for sb in range(nb):
            for d in (0, 1):
                for q in range(PAIRS):
                    blk = init_ref[sb, d, 2 * q:2 * q + 2].reshape(2 * p, SSD_D_STATE)
                    st_scr[sb, d, :, q * LANES:(q + 1) * LANES] = blk.T

    def chunk_body(i, carry, first, zero_state=False):
        for sb, d in [(sb, d) for sb in range(nb) for d in (0, 1)]:
            causal = (row >= col) if d == 0 else (row <= col)
            c = sb * nc + (i if d == 0 else nc - 1 - i)
            r0 = c * CHUNK if isinstance(c, int) else pl.multiple_of(c * CHUNK, CHUNK)
            rows = pl.ds(r0, CHUNK)
            bc = b_ref[rows, :]
            cc = c_ref[rows, :]
            g = lax.dot_general(cc, bc, (((1,), (1,)), ((), ())), preferred_element_type=F32)
            gm = jnp.where(causal, g, 0.0)
            bt = bc.astype(F32).T
            cols = col_ref[rows, :]
            edge = CHUNK - 1 if d == 0 else 0
            if not zero_state:
                y_off = _dot(cc, st_scr[sb, d].astype(BF16))
            spread = _dot(cols, bcast_ref[d])
            xbd_q, m_q, bt_q, e_q = [], [], [], []
            for q in range(PAIRS):
                x = xs_ref[rows, q * LANES:(q + 1) * LANES]
                zero = jnp.zeros_like(x)
                xbd_q.append(jnp.concatenate([jnp.where(lo_half, x, zero), jnp.where(lo_half, zero, x)], axis=0))
                m_h, bt_h, e_h = [], [], []
                for h in (2 * q, 2 * q + 1):
                    j = d * HEADS_PER_GROUP + h
                    cs_l = spread[:, h * LANES:(h + 1) * LANES]
                    csd_s = row_ref[c, pl.ds(j, 1), :]
                    ldt_s = row_ref[c, pl.ds(N_DH + j, 1), :]
                    w_s = row_ref[c, pl.ds(2 * N_DH + j, 1), :]
                    lmat_dt = jnp.exp2(jnp.minimum(cs_l - csd_s, ldt_s))
                    m_h.append((gm * lmat_dt).astype(BF16))
                    bt_h.append((bt * w_s).astype(BF16))
                    e_h.append(cs_l)
                m_q.append(jnp.concatenate(m_h, axis=1))
                bt_q.append(jnp.concatenate(bt_h, axis=1))
                if not zero_state:
                    e_q.append(jnp.exp2(jnp.where(lo_half, e_h[0], e_h[1])))
            y_diag = [_dot(m_q[q], xbd_q[q]) for q in range(PAIRS)]
            s_new = [_dot(bt_q[q], xbd_q[q]) for q in range(PAIRS)]
            for q in range(PAIRS):
                lanes = slice(q * LANES, (q + 1) * LANES)
                y = y_diag[q] if zero_state else y_diag[q] + y_off[:, lanes] * e_q[q]
                if first:
                    y_scr[rows, lanes] = y + dsk_ref[:, lanes] * xs_ref[rows, lanes].astype(F32)
                else:
                    y_ref[rows, lanes] = (y_scr[rows, lanes] + y).astype(y_ref.dtype)
                if zero_state:
                    st_scr[sb, d, :, lanes] = s_new[q]
                else:
                    st_scr[sb, d, :, lanes] = (st_scr[sb, d, :, lanes] * e_q[q][edge:edge + 1, :]
                                               + s_new[q])
        return carry

    assert nc % 2 == 0
    start = 0
    if not has_init:
        chunk_body(0, 0, first=True, zero_state=True)
        start = 1
    lax.fori_loop(start, nc // 2, functools.partial(chunk_body, first=True), 0)
    lax.fori_loop(nc // 2, nc, functools.partial(chunk_body, first=False), 0)

    if want_final:
        for sb in range(nb):
            for d in (0, 1):
                for q in range(PAIRS):
                    lanes = slice(q * LANES, (q + 1) * LANES)
                    fin_ref[sb, d, 2 * q:2 * q + 2] = st_scr[sb, d, :, lanes].T.reshape(2, p, SSD_D_STATE)


def _ssd_scan(xbc, cols_g, rows_g, d_skip, init, batch, seq, *, want_final):
    nc = seq // CHUNK
    width = SSD_GROUPS * GROUP_W
    heads = SSD_GROUPS * HEADS_PER_GROUP
    has_init = init is not None
    bcast = _lane_bcast_matrices()
    nb = SSD_SEQS_PER_STEP * (2 if nc <= 2 else 1)
    assert batch % nb == 0
    rows = nb * seq
    in_specs = [
        pl.BlockSpec((rows, GROUP_W), lambda b, g: (b, g)),
        pl.BlockSpec((rows, SSD_D_STATE), lambda b, g: (b, width // SSD_D_STATE + g)),
        pl.BlockSpec((rows, SSD_D_STATE), lambda b, g: (b, width // SSD_D_STATE + SSD_GROUPS + g)),
        pl.BlockSpec((None, rows, LANES), lambda b, g: (g, b, 0)),
        pl.BlockSpec((None, nb * nc, 3 * N_DH, CHUNK), lambda b, g: (g, b, 0, 0)),
        pl.BlockSpec(bcast.shape, lambda b, g: (0, 0, 0)),
        pl.BlockSpec((1, GROUP_W), lambda b, g: (0, g)),
    ]
    args = [xbc, xbc, xbc, cols_g, rows_g, bcast, d_skip]
    state_spec = pl.BlockSpec((nb, 2, HEADS_PER_GROUP, SSD_HEAD_DIM, SSD_D_STATE),
                              lambda b, g: (b, 0, g, 0, 0))
    if has_init:
        in_specs.append(state_spec)
        args.append(init)
    out_specs = [pl.BlockSpec((rows, GROUP_W), lambda b, g: (b, g))]
    out_shape = [jax.ShapeDtypeStruct((batch * seq, width), BF16)]
    if want_final:
        out_specs.append(state_spec)
        out_shape.append(jax.ShapeDtypeStruct((batch, 2, heads, SSD_HEAD_DIM, SSD_D_STATE), F32))
    out = pl.pallas_call(
        functools.partial(_ssd_kernel, nc=nc, nb=nb, has_init=has_init, want_final=want_final),
        grid=(batch // nb, SSD_GROUPS),
        in_specs=in_specs,
        out_specs=out_specs,
        out_shape=out_shape,
        scratch_shapes=[pltpu.VMEM((nb, 2, SSD_D_STATE, GROUP_W), F32), pltpu.VMEM((rows, GROUP_W), F32)],
        compiler_params=_params(("parallel", "parallel"), 48),
        name="ssd_scan",
    )(*args)
    return out if want_final else (out[0], None)


def _trunk(x, pos, batch, seq, per_seq_cond, init, mod5, wts, *, want_final):
    (ln_g, ln_b, fno_w_in, fno_w_out, ssd_w_in, conv_w, conv_b, dt_bias, a_log,
     d_skip, norm_w, ssd_w_out) = wts
    fno_width = fno_w_out.shape[0]
    ssd_width = ssd_w_out.shape[0]
    tm_in, tm_out, tm_gated = 1024, 512, 256

    def cond_row(tm):
        if not per_seq_cond:
            return lambda i: CTX_ROW
        assert seq % tm == 0
        return lambda i: (i * tm) // seq

    tn0 = 2048 if pos is None else 1024
    uz = _ln_in_proj(x, pos, mod5, 0, cond_row(tm_in), fno_w_in, False, tm=tm_in, tn=tn0)
    yg = _fno_core(uz, batch, seq, fno_width, tw=min(fno_width, 4096 * 256 // seq))
    x1 = _out_proj_ln(yg, None, None, fno_w_out, x, pos, mod5, 0, cond_row(tm_out), ln_g, ln_b, tm=tm_out)

    zx, dt_raw = _ln_in_proj(x1, None, mod5, 1, cond_row(tm_in), ssd_w_in, True, tm=tm_in, tn=2048)
    xbc = _conv_silu(zx, conv_w, conv_b, batch, seq, ssd_width, tc=2048)
    cols_g, rows_g = _dt_prep(dt_raw, dt_bias, a_log, n_ch=8)
    y, fin = _ssd_scan(xbc, cols_g, rows_g, d_skip, init, batch, seq, want_final=want_final)
    x2 = _out_proj_ln(y, zx, norm_w, ssd_w_out, x1, None, mod5, 1, cond_row(tm_gated), ln_g, ln_b, tm=tm_gated)
    return x2, fin


def _sincos(pos, dim):
    omega = 1.0 / (10000.0 ** (np.arange(dim // 2, dtype=np.float64) / (dim / 2)))
    ang = pos.astype(np.float64)[:, None] * omega[None, :]
    return np.concatenate([np.sin(ang), np.cos(ang)], axis=-1)


def _grid_pos_embed(n_tokens, dim):
    t = np.arange(n_tokens)
    return np.concatenate([_sincos(t // GRID_W, dim // 2), _sincos(t % GRID_W, dim // 2)], axis=-1)


def kernel(x_prompt, x_sample, state_ssd_ctx, c, c_ctx, w_ada, b_ada, ln_g, ln_b, fno_w_in, fno_w_out,
           ssd_w_in, ssd_conv_w, ssd_conv_b, ssd_dt_bias, ssd_a_log, ssd_d, ssd_norm_w, ssd_w_out):
    batch, seq, d = x_prompt.shape
    dec_batch, dec_seq, _ = x_sample.shape
    ssd_width = ssd_w_out.shape[1]
    conv_dim = ssd_conv_w.shape[2]
    assert dec_batch <= CTX_ROW

    cond = jnp.zeros((COND_ROWS, d), F32).at[:dec_batch].set(c).at[CTX_ROW].set(c_ctx)
    mod = _modulation(cond, w_ada, b_ada)
    mod5 = mod.reshape(DEPTH, COND_ROWS, 3, 1, d)

    assert ssd_w_in.shape[2] == ssd_width + conv_dim + LANES
    wts = (ln_g, ln_b,
           fno_w_in[0].astype(BF16), fno_w_out[0].astype(BF16), ssd_w_in[0].astype(BF16),
           ssd_conv_w[0], ssd_conv_b[0], ssd_dt_bias[0], ssd_a_log[0],
           jnp.repeat(ssd_d[0], SSD_HEAD_DIM).reshape(1, ssd_width),
           ssd_norm_w[0], ssd_w_out[0].astype(BF16))

    y_prompt, fin = _trunk(x_prompt.reshape(batch * seq, d), None, batch, seq,
                           False, None, mod5, wts, want_final=True)
    pos = jnp.asarray(_grid_pos_embed(dec_seq, d), dtype=F32)
    y_sample, _ = _trunk(x_sample.reshape(dec_batch * dec_seq, d), pos, dec_batch, dec_seq,
                         True, state_ssd_ctx[:, 0], mod5, wts, want_final=False)
    return (y_prompt.reshape(batch, seq, d), y_sample.reshape(dec_batch, dec_seq, d),
            fin[:, None])
```

```python
import functools
import math

import numpy as np
import jax
import jax.numpy as jnp
from jax import lax
from jax.experimental import pallas as pl
from jax.experimental.pallas import tpu as pltpu

F32 = jnp.float32
BF16 = jnp.bfloat16

DEPTH = 2
GRID_W = 64
FNO_GROUP_DIM = 256
SSD_HEAD_DIM = 64
SSD_GROUPS = 8
SSD_D_STATE = 128
SSD_CONV_W = 5
CHUNK = 128
DEEPNORM_ALPHA = (2 * DEPTH) ** 0.25
LN_EPS = 1e-5
COND_ROWS = 16
CTX_ROW = 8
LANES = 128
SUBLANES = 8
MIB = 1024 * 1024


def _params(semantics, vmem_mib):
    return pltpu.CompilerParams(dimension_semantics=semantics, vmem_limit_bytes=vmem_mib * MIB)


def _silu(x):
    h = 0.5 * x
    return h + h * jnp.tanh(h)


def _dot(a, b):
    return jnp.dot(a, b, preferred_element_type=F32)


def _mod_kernel(c_ref, w_ref, b_ref, o_ref):
    s = _silu(c_ref[...]).astype(BF16)
    o_ref[...] = _dot(s, w_ref[...].astype(BF16)) + b_ref[...]


def _modulation(cond, w_ada, b_ada):
    depth, d, n = w_ada.shape
    tn = 512
    return pl.pallas_call(
        _mod_kernel,
        grid=(depth, n // tn),
        in_specs=[
            pl.BlockSpec((COND_ROWS, d), lambda i, j: (0, 0)),
            pl.BlockSpec((None, d, tn), lambda i, j: (i, 0, j)),
            pl.BlockSpec((None, 1, tn), lambda i, j: (i, 0, j)),
        ],
        out_specs=pl.BlockSpec((None, COND_ROWS, tn), lambda i, j: (i, 0, j)),
        out_shape=jax.ShapeDtypeStruct((depth, COND_ROWS, n), F32),
        compiler_params=_params(("parallel", "parallel"), 32),
        name="adaln_mod",
    )(cond, w_ada, b_ada.reshape(depth, 1, n))


LN_ROWS = 256


def _ln_modulate(x, shift, scale):
    mu = jnp.mean(x, axis=-1, keepdims=True)
    xc = x - mu
    var = jnp.mean(xc * xc, axis=-1, keepdims=True)
    return xc * lax.rsqrt(var + LN_EPS) * (1.0 + scale) + shift


def _ln_mm_kernel(*refs, has_pos, has_dt):
    refs = list(refs)
    x_ref = refs.pop(0)
    pos_ref = refs.pop(0) if has_pos else None
    shift_ref, scale_ref, w_ref = refs.pop(0), refs.pop(0), refs.pop(0)
    wdt_ref = refs.pop(0) if has_dt else None
    o_ref = refs.pop(0)
    dt_ref = refs.pop(0) if has_dt else None
    h_scr = refs.pop(0)

    @pl.when(pl.program_id(1) == 0)
    def _():
        piece = min(LN_ROWS, x_ref.shape[0])
        for r in range(0, x_ref.shape[0], piece):
            rows = slice(r, r + piece)
            x = x_ref[rows, :]
            if has_pos:
                x = x + pos_ref[rows, :]
            h = _ln_modulate(x, shift_ref[...], scale_ref[...]).astype(BF16)
            h_scr[rows, :] = h
            if has_dt:
                dt_ref[rows, :] = _dot(h, wdt_ref[...])

    o_ref[...] = _dot(h_scr[...], w_ref[...]).astype(o_ref.dtype)


def _ln_in_proj(x, pos, mod5, layer, cond_row, w, with_dt, *, tm, tn):
    t, d = x.shape
    has_pos, has_dt = pos is not None, with_dt
    n = w.shape[1] - (LANES if has_dt else 0)
    in_specs = [pl.BlockSpec((tm, d), lambda i, j: (i, 0))]
    args = [x]
    if has_pos:
        pos_blocks = pos.shape[0] // tm
        mode = dict(pipeline_mode=pl.Buffered(1)) if pos_blocks == 1 else {}
        in_specs.append(pl.BlockSpec((tm, d), lambda i, j: (i % pos_blocks, 0), **mode))
        args.append(pos)
    for part in (0, 1):
        in_specs.append(pl.BlockSpec((None, None, None, 1, d),
                                     lambda i, j, part=part: (layer, cond_row(i), part, 0, 0)))
        args.append(mod5)
    in_specs.append(pl.BlockSpec((d, tn), lambda i, j: (0, j)))
    args.append(w)
    out_specs = [pl.BlockSpec((tm, tn), lambda i, j: (i, j))]
    out_shape = [jax.ShapeDtypeStruct((t, n), BF16)]
    if has_dt:
        in_specs.append(pl.BlockSpec((d, LANES), lambda i, j: (0, n // LANES)))
        args.append(w)
        out_specs.append(pl.BlockSpec((tm, LANES), lambda i, j: (i, 0)))
        out_shape.append(jax.ShapeDtypeStruct((t, LANES), F32))
    out = pl.pallas_call(
        functools.partial(_ln_mm_kernel, has_pos=has_pos, has_dt=has_dt),
        grid=(t // tm, n // tn),
        in_specs=in_specs,
        out_specs=out_specs,
        out_shape=out_shape,
        scratch_shapes=[pltpu.VMEM((tm, d), BF16)],
        compiler_params=_params(("parallel", "arbitrary"), 56),
        name="ln_in_proj",
    )(*args)
    return out if has_dt else out[0]


DFT_TAIL = 16


def _dft_matrices(seq, group_dim):
    def cos_sin(n):
        k = np.arange(n, dtype=np.int64)
        ang = 2.0 * np.pi * ((k[:, None] * k[None, :]) % n).astype(np.float64) / n
        return np.cos(ang), np.sin(ang)
    half = seq // 2
    cl, sl = cos_sin(seq)
    cc, sc = cos_sin(group_dim)
    norm = 1.0 / math.sqrt(seq * group_dim)
    posm = np.zeros((2 * half + DFT_TAIL, seq))
    posm[:half] = cl[:half] * norm
    posm[half:2 * half] = sl[:half] * norm
    posm[2 * half] = cl[half] * norm
    flip = np.zeros((half, half))
    flip[0, 0] = 1.0
    flip[np.arange(1, half), half - np.arange(1, half)] = 1.0
    as_bf16 = lambda a: jnp.asarray(a, dtype=F32).astype(BF16)
    return as_bf16(posm), as_bf16(np.stack([cc, sc])), as_bf16(flip)


def _fno_kernel(u_ref, z_ref, posm_ref, chan_ref, flip_ref, o_ref):
    seq, tw = u_ref.shape
    half = seq // 2
    cg = FNO_GROUP_DIM
    first_row = lax.broadcasted_iota(jnp.int32, (half, cg), 0) == 0
    groups = [slice(k * cg, (k + 1) * cg) for k in range(tw // cg)]
    ab = _dot(posm_ref[...], u_ref[...]).astype(BF16)
    p = [_dot(ab[:half, cols], chan_ref[0]) for cols in groups]
    q = [_dot(ab[half:2 * half, cols], chan_ref[1]) for cols in groups]
    p_nyq = [_dot(ab[2 * half:, cols], chan_ref[0])[0:1, :] for cols in groups]
    mirrored = jnp.concatenate(
        [jnp.where(first_row, p_nyq[k], p[k] + q[k]).astype(BF16) for k in range(len(groups))], axis=1)
    bottom = _dot(flip_ref[...], mirrored)
    for k, cols in enumerate(groups):
        gate = _silu(z_ref[:, cols].astype(F32))
        o_ref[:half, cols] = ((p[k] - q[k]) * gate[:half]).astype(o_ref.dtype)
        o_ref[half:, cols] = (bottom[:, cols] * gate[half:]).astype(o_ref.dtype)


def _fno_core(uz, batch, seq, width, *, tw):
    consts = _dft_matrices(seq, FNO_GROUP_DIM)
    nw = width // tw
    const_specs = [pl.BlockSpec(c.shape, lambda b, j, nd=c.ndim: (0,) * nd) for c in consts]
    return pl.pallas_call(
        _fno_kernel,
        grid=(batch, nw),
        in_specs=[
            pl.BlockSpec((seq, tw), lambda b, j: (b, j)),
            pl.BlockSpec((seq, tw), lambda b, j: (b, nw + j)),
        ] + const_specs,
        out_specs=pl.BlockSpec((seq, tw), lambda b, j: (b, j)),
        out_shape=jax.ShapeDtypeStruct((batch * seq, width), BF16),
        compiler_params=_params(("parallel", "parallel"), 48),
        name="fno_core",
    )(uz, uz, *consts)


def _out_ln_kernel(*refs, has_pos, gated):
    refs = list(refs)
    a_ref = refs.pop(0)
    z_ref, nw_ref = (refs.pop(0), refs.pop(0)) if gated else (None, None)
    w_ref, x_ref = refs.pop(0), refs.pop(0)
    pos_ref = refs.pop(0) if has_pos else None
    gate_ref, g_ref, b_ref, o_ref = refs

    if gated:
        y = a_ref[...].astype(F32) * _silu(z_ref[...].astype(F32))
        inv_rms = lax.rsqrt(jnp.mean(y * y, axis=-1, keepdims=True) + LN_EPS)
        acc = _dot((y * nw_ref[...]).astype(BF16), w_ref[...]) * inv_rms
    else:
        acc = _dot(a_ref[...], w_ref[...])
    x = x_ref[...]
    if has_pos:
        x = x + pos_ref[...]
    r = DEEPNORM_ALPHA * x + gate_ref[...] * acc
    mu = jnp.mean(r, axis=-1, keepdims=True)
    rc = r - mu
    var = jnp.mean(rc * rc, axis=-1, keepdims=True)
    o_ref[...] = rc * lax.rsqrt(var + LN_EPS) * g_ref[...] + b_ref[...]


def _out_proj_ln(a, z, norm_w, w, x, pos, mod5, layer, cond_row, ln_g, ln_b, *, tm):
    t, kdim = a.shape
    d = w.shape[1]
    has_pos, gated = pos is not None, z is not None
    row_k = pl.BlockSpec((tm, kdim), lambda i: (i, 0))
    row_d = pl.BlockSpec((tm, d), lambda i: (i, 0))
    in_specs, args = [row_k], [a]
    if gated:
        in_specs += [row_k, pl.BlockSpec((1, kdim), lambda i: (0, 0))]
        args += [z, norm_w.reshape(1, kdim)]
    in_specs += [pl.BlockSpec((kdim, d), lambda i: (0, 0), pipeline_mode=pl.Buffered(1)), row_d]
    args += [w, x]
    if has_pos:
        pos_blocks = pos.shape[0] // tm
        in_specs.append(pl.BlockSpec((tm, d), lambda i: (i % pos_blocks, 0)))
        args.append(pos)
    in_specs.append(pl.BlockSpec((None, None, None, 1, d), lambda i: (layer, cond_row(i), 2, 0, 0)))
    args.append(mod5)
    for v in (ln_g, ln_b):
        in_specs.append(pl.BlockSpec((None, 1, d), lambda i: (layer, 0, 0)))
        args.append(v.reshape(v.shape[0], 1, d))
    return pl.pallas_call(
        functools.partial(_out_ln_kernel, has_pos=has_pos, gated=gated),
        grid=(t // tm,),
        in_specs=in_specs,
        out_specs=row_d,
        out_shape=jax.ShapeDtypeStruct((t, d), F32),
        compiler_params=_params(("parallel",), 56),
        name="out_proj_ln",
    )(*args)


CONV_OFFSETS = (-2, -1, 1, 2)
CONV_PAD = 16
CONV_STRIP = 256
CONV_TILE = 64


def _shift_matrix():
    tile = CONV_TILE
    s = np.zeros((len(CONV_OFFSETS) * tile, tile + 2 * CONV_PAD), np.float32)
    for t, off in enumerate(CONV_OFFSETS):
        s[t * tile + np.arange(tile), CONV_PAD + np.arange(tile) + off] = 1.0
    return jnp.asarray(s, dtype=BF16)


def _conv_kernel(x_ref, s_ref, w_ref, b_ref, o_ref, pad_scr):
    seq, tc = x_ref.shape
    tile = CONV_TILE
    pad_scr[0:CONV_PAD, :] = jnp.zeros((CONV_PAD, tc), BF16)
    pad_scr[CONV_PAD + seq:2 * CONV_PAD + seq, :] = jnp.zeros((CONV_PAD, tc), BF16)
    pad_scr[CONV_PAD:CONV_PAD + seq, :] = x_ref[...]
    centre = SSD_CONV_W // 2

    def rows_of(ref, k, lanes):
        return jnp.tile(ref[k * SUBLANES:(k + 1) * SUBLANES, lanes], (tile // SUBLANES, 1))

    for i in range(seq // tile):
        for c in range(0, tc, CONV_STRIP):
            lanes = slice(c, c + CONV_STRIP)
            window = pad_scr[i * tile:(i + 1) * tile + 2 * CONV_PAD, lanes]
            taps = _dot(s_ref[...], window)
            acc = (rows_of(b_ref, 0, lanes)
                   + window[CONV_PAD:CONV_PAD + tile, :].astype(F32) * rows_of(w_ref, centre, lanes))
            for t, off in enumerate(CONV_OFFSETS):
                acc = acc + taps[t * tile:(t + 1) * tile, :] * rows_of(w_ref, centre + off, lanes)
            o_ref[i * tile:(i + 1) * tile, lanes] = _silu(acc).astype(o_ref.dtype)


def _conv_silu(zx, conv_w, conv_b, batch, seq, col0, *, tc):
    conv_dim = conv_w.shape[1]
    assert col0 % tc == 0 and conv_dim % tc == 0
    c0 = col0 // tc
    w_rows = jnp.repeat(conv_w, SUBLANES, axis=0)
    b_rows = jnp.broadcast_to(conv_b.reshape(1, conv_dim), (SUBLANES, conv_dim))
    shift = _shift_matrix()
    return pl.pallas_call(
        _conv_kernel,
        grid=(batch, conv_dim // tc),
        in_specs=[pl.BlockSpec((seq, tc), lambda b, j: (b, c0 + j)),
                  pl.BlockSpec(shift.shape, lambda b, j: (0, 0)),
                  pl.BlockSpec((SSD_CONV_W * SUBLANES, tc), lambda b, j: (0, j)),
                  pl.BlockSpec((SUBLANES, tc), lambda b, j: (0, j))],
        out_specs=pl.BlockSpec((seq, tc), lambda b, j: (b, j)),
        out_shape=jax.ShapeDtypeStruct((batch * seq, conv_dim), BF16),
        scratch_shapes=[pltpu.VMEM((seq + 2 * CONV_PAD, tc), BF16)],
        compiler_params=_params(("parallel", "parallel"), 48),
        name="ssd_conv_silu",
    )(zx, shift, w_rows, b_rows)


def _split3(x):
    hi = x.astype(BF16)
    r = x - hi.astype(F32)
    mid = r.astype(BF16)
    lo = (r - mid.astype(F32)).astype(BF16)
    return hi, mid, lo


HEADS_PER_GROUP = 8
PAIRS = HEADS_PER_GROUP // 2
GROUP_W = HEADS_PER_GROUP * SSD_HEAD_DIM
N_DH = 2 * HEADS_PER_GROUP
SPLIT = 3
LOG2E = 1.4426950408889634


def _group_select_matrix():
    sel = np.zeros((SPLIT * LANES, SSD_GROUPS * LANES), np.float32)
    half = LANES // 2
    for g in range(SSD_GROUPS):
        for j in range(N_DH):
            d, h = divmod(j, HEADS_PER_GROUP)
            src = d * half + g * HEADS_PER_GROUP + h
            for p in range(SPLIT):
                sel[p * LANES + src, g * LANES + SPLIT * j + p] = 1.0
    return jnp.asarray(sel, dtype=BF16)


def _lane_bcast_matrices():
    e = np.zeros((2, LANES, HEADS_PER_GROUP * LANES), np.float32)
    for d in range(2):
        for h in range(HEADS_PER_GROUP):
            j = d * HEADS_PER_GROUP + h
            e[d, SPLIT * j:SPLIT * (j + 1), h * LANES:(h + 1) * LANES] = 1.0
    return jnp.asarray(e, dtype=BF16)


def _dt_kernel(raw_ref, bias_ref, alog_ref, sel_ref, col_ref, row_ref):
    n_ch = row_ref.shape[1]
    half = LANES // 2
    row = lax.broadcasted_iota(jnp.int32, (CHUNK, CHUNK), 0)
    col = lax.broadcasted_iota(jnp.int32, (CHUNK, CHUNK), 1)
    tri_f = jnp.where(row >= col, 1.0, 0.0).astype(BF16)
    tri_b = jnp.where(row <= col, 1.0, 0.0).astype(BF16)
    is_fwd = col < half
    a2 = -jnp.exp(alog_ref[...]) * LOG2E
    for k in range(n_ch):
        rows = slice(k * CHUNK, (k + 1) * CHUNK)
        x = raw_ref[rows, :] + bias_ref[...]
        dt = jnp.maximum(x, 0.0) + jnp.log1p(jnp.exp(-jnp.abs(x)))
        parts = _split3(dt * a2)
        cs = jnp.where(is_fwd, sum(_dot(tri_f, p) for p in parts), sum(_dot(tri_b, p) for p in parts))
        tot = jnp.where(is_fwd[:1], cs[CHUNK - 1:CHUNK, :], cs[0:1, :])
        w = dt * jnp.exp2(tot - cs)
        pieces = jnp.concatenate(_split3(cs), axis=1)
        grouped = _dot(pieces, sel_ref[...]).astype(BF16)
        log_dt = jnp.log2(dt)
        transposed = ((cs - log_dt).T, log_dt.T, w.T)
        for g in range(SSD_GROUPS):
            col_ref[g, rows, :] = grouped[:, g * LANES:(g + 1) * LANES]
            for n, v in enumerate(transposed):
                for d in (0, 1):
                    src = d * half + g * HEADS_PER_GROUP
                    dst = n * N_DH + d * HEADS_PER_GROUP
                    row_ref[g, k, dst:dst + HEADS_PER_GROUP, :] = v[src:src + HEADS_PER_GROUP, :]


def _dt_prep(dt_raw, dt_bias, a_log, *, n_ch):
    t = dt_raw.shape[0]
    tm = n_ch * CHUNK
    sel = _group_select_matrix()
    vec_spec = pl.BlockSpec((1, LANES), lambda i: (0, 0))
    return pl.pallas_call(
        _dt_kernel,
        grid=(t // tm,),
        in_specs=[pl.BlockSpec((tm, LANES), lambda i: (i, 0)), vec_spec, vec_spec,
                  pl.BlockSpec(sel.shape, lambda i: (0, 0))],
        out_specs=[pl.BlockSpec((SSD_GROUPS, tm, LANES), lambda i: (0, i, 0)),
                   pl.BlockSpec((SSD_GROUPS, n_ch, 3 * N_DH, CHUNK), lambda i: (0, i, 0, 0))],
        out_shape=[jax.ShapeDtypeStruct((SSD_GROUPS, t, LANES), BF16),
                   jax.ShapeDtypeStruct((SSD_GROUPS, t // CHUNK, 3 * N_DH, CHUNK), F32)],
        compiler_params=_params(("parallel",), 32),
        name="ssd_dt_prep",
    )(dt_raw, dt_bias.reshape(1, LANES), a_log.reshape(1, LANES), sel)


SSD_SEQS_PER_STEP = 2


def _ssd_kernel(*refs, nc, nb, has_init, want_final):
    refs = list(refs)
    xs_ref, b_ref, c_ref, col_ref, row_ref, bcast_ref, dsk_ref = refs[:7]
    del refs[:7]
    init_ref = refs.pop(0) if has_init else None
    y_ref = refs.pop(0)
    fin_ref = refs.pop(0) if want_final else None
    st_scr, y_scr = refs

    p = SSD_HEAD_DIM
    row = lax.broadcasted_iota(jnp.int32, (CHUNK, CHUNK), 0)
    col = lax.broadcasted_iota(jnp.int32, (CHUNK, CHUNK), 1)
    lo_half = col < p

    if has_init:
        for sb in range(nb):
            for d in (0, 1):
                for q in range(PAIRS):
                    blk = init_ref[sb, d, 2 * q:2 * q + 2].reshape(2 * p, SSD_D_STATE)
                    st_scr[sb, d, :, q * LANES:(q + 1) * LANES] = blk.T

    def chunk_body(i, carry, first, zero_state=False):
        for sb, d in [(sb, d) for sb in range(nb) for d in (0, 1)]:
            causal = (row >= col) if d == 0 else (row <= col)
            c = sb * nc + (i if d == 0 else nc - 1 - i)
            r0 = c * CHUNK if isinstance(c, int) else pl.multiple_of(c * CHUNK, CHUNK)
            rows = pl.ds(r0, CHUNK)
            bc = b_ref[rows, :]
            cc = c_ref[rows, :]
            g = lax.dot_general(cc, bc, (((1,), (1,)), ((), ())), preferred_element_type=F32)
            gm = jnp.where(causal, g, 0.0)
            bt = bc.astype(F32).T
            cols = col_ref[rows, :]
            edge = CHUNK - 1 if d == 0 else 0
            if not zero_state:
                y_off = _dot(cc, st_scr[sb, d].astype(BF16))
            spread = _dot(cols, bcast_ref[d])
            xbd_q, m_q, bt_q, e_q = [], [], [], []
            for q in range(PAIRS):
                x = xs_ref[rows, q * LANES:(q + 1) * LANES]
                zero = jnp.zeros_like(x)
                xbd_q.append(jnp.concatenate([jnp.where(lo_half, x, zero), jnp.where(lo_half, zero, x)], axis=0))
                m_h, bt_h, e_h = [], [], []
                for h in (2 * q, 2 * q + 1):
                    j = d * HEADS_PER_GROUP + h
                    cs_l = spread[:, h * LANES:(h + 1) * LANES]
                    csd_s = row_ref[c, pl.ds(j, 1), :]
                    ldt_s = row_ref[c, pl.ds(N_DH + j, 1), :]
                    w_s = row_ref[c, pl.ds(2 * N_DH + j, 1), :]
                    lmat_dt = jnp.exp2(jnp.minimum(cs_l - csd_s, ldt_s))
                    m_h.append((gm * lmat_dt).astype(BF16))
                    bt_h.append((bt * w_s).astype(BF16))
                    e_h.append(cs_l)
                m_q.append(jnp.concatenate(m_h, axis=1))
                bt_q.append(jnp.concatenate(bt_h, axis=1))
                if not zero_state:
                    e_q.append(jnp.exp2(jnp.where(lo_half, e_h[0], e_h[1])))
            y_diag = [_dot(m_q[q], xbd_q[q]) for q in range(PAIRS)]
            s_new = [_dot(bt_q[q], xbd_q[q]) for q in range(PAIRS)]
            for q in range(PAIRS):
                lanes = slice(q * LANES, (q + 1) * LANES)
                y = y_diag[q] if zero_state else y_diag[q] + y_off[:, lanes] * e_q[q]
                if first:
                    y_scr[rows, lanes] = y + dsk_ref[:, lanes] * xs_ref[rows, lanes].astype(F32)
                else:
                    y_ref[rows, lanes] = (y_scr[rows, lanes] + y).astype(y_ref.dtype)
                if zero_state:
                    st_scr[sb, d, :, lanes] = s_new[q]
                else:
                    st_scr[sb, d, :, lanes] = (st_scr[sb, d, :, lanes] * e_q[q][edge:edge + 1, :]
                                               + s_new[q])
        return carry

    assert nc % 2 == 0
    start = 0
    if not has_init:
        chunk_body(0, 0, first=True, zero_state=True)
        start = 1
    lax.fori_loop(start, nc // 2, functools.partial(chunk_body, first=True), 0)
    lax.fori_loop(nc // 2, nc, functools.partial(chunk_body, first=False), 0)

    if want_final:
        for sb in range(nb):
            for d in (0, 1):
                for q in range(PAIRS):
                    lanes = slice(q * LANES, (q + 1) * LANES)
                    fin_ref[sb, d, 2 * q:2 * q + 2] = st_scr[sb, d, :, lanes].T.reshape(2, p, SSD_D_STATE)


def _ssd_scan(xbc, cols_g, rows_g, d_skip, init, batch, seq, *, want_final):
    nc = seq // CHUNK
    width = SSD_GROUPS * GROUP_W
    heads = SSD_GROUPS * HEADS_PER_GROUP
    has_init = init is not None
    bcast = _lane_bcast_matrices()
    nb = SSD_SEQS_PER_STEP * (2 if nc <= 2 else 1)
    assert batch % nb == 0
    rows = nb * seq
    in_specs = [
        pl.BlockSpec((rows, GROUP_W), lambda b, g: (b, g)),
        pl.BlockSpec((rows, SSD_D_STATE), lambda b, g: (b, width // SSD_D_STATE + g)),
        pl.BlockSpec((rows, SSD_D_STATE), lambda b, g: (b, width // SSD_D_STATE + SSD_GROUPS + g)),
        pl.BlockSpec((None, rows, LANES), lambda b, g: (g, b, 0)),
        pl.BlockSpec((None, nb * nc, 3 * N_DH, CHUNK), lambda b, g: (g, b, 0, 0)),
        pl.BlockSpec(bcast.shape, lambda b, g: (0, 0, 0)),
        pl.BlockSpec((1, GROUP_W), lambda b, g: (0, g)),
    ]
    args = [xbc, xbc, xbc, cols_g, rows_g, bcast, d_skip]
    state_spec = pl.BlockSpec((nb, 2, HEADS_PER_GROUP, SSD_HEAD_DIM, SSD_D_STATE),
                              lambda b, g: (b, 0, g, 0, 0))
    if has_init:
        in_specs.append(state_spec)
        args.append(init)
    out_specs = [pl.BlockSpec((rows, GROUP_W), lambda b, g: (b, g))]
    out_shape = [jax.ShapeDtypeStruct((batch * seq, width), BF16)]
    if want_final:
        out_specs.append(state_spec)
        out_shape.append(jax.ShapeDtypeStruct((batch, 2, heads, SSD_HEAD_DIM, SSD_D_STATE), F32))
    out = pl.pallas_call(
        functools.partial(_ssd_kernel, nc=nc, nb=nb, has_init=has_init, want_final=want_final),
        grid=(batch // nb, SSD_GROUPS),
        in_specs=in_specs,
        out_specs=out_specs,
        out_shape=out_shape,
        scratch_shapes=[pltpu.VMEM((nb, 2, SSD_D_STATE, GROUP_W), F32), pltpu.VMEM((rows, GROUP_W), F32)],
        compiler_params=_params(("parallel", "parallel"), 48),
        name="ssd_scan",
    )(*args)
    return out if want_final else (out[0], None)


def _trunk(x, pos, batch, seq, per_seq_cond, init, mod5, wts, *, want_final):
    (ln_g, ln_b, fno_w_in, fno_w_out, ssd_w_in, conv_w, conv_b, dt_bias, a_log,
     d_skip, norm_w, ssd_w_out) = wts
    fno_width = fno_w_out.shape[0]
    ssd_width = ssd_w_out.shape[0]
    tm_in, tm_out, tm_gated = 1024, 512, 256

    def cond_row(tm):
        if not per_seq_cond:
            return lambda i: CTX_ROW
        assert seq % tm == 0
        return lambda i: (i * tm) // seq

    tn0 = 2048 if pos is None else 1024
    uz = _ln_in_proj(x, pos, mod5, 0, cond_row(tm_in), fno_w_in, False, tm=tm_in, tn=tn0)
    yg = _fno_core(uz, batch, seq, fno_width, tw=min(fno_width, 4096 * 256 // seq))
    x1 = _out_proj_ln(yg, None, None, fno_w_out, x, pos, mod5, 0, cond_row(tm_out), ln_g, ln_b, tm=tm_out)

    zx, dt_raw = _ln_in_proj(x1, None, mod5, 1, cond_row(tm_in), ssd_w_in, True, tm=tm_in, tn=2048)
    xbc = _conv_silu(zx, conv_w, conv_b, batch, seq, ssd_width, tc=2048)
    cols_g, rows_g = _dt_prep(dt_raw, dt_bias, a_log, n_ch=8)
    y, fin = _ssd_scan(xbc, cols_g, rows_g, d_skip, init, batch, seq, want_final=want_final)
    x2 = _out_proj_ln(y, zx, norm_w, ssd_w_out, x1, None, mod5, 1, cond_row(tm_gated), ln_g, ln_b, tm=tm_gated)
    return x2, fin


def _sincos(pos, dim):
    omega = 1.0 / (10000.0 ** (np.arange(dim // 2, dtype=np.float64) / (dim / 2)))
    ang = pos.astype(np.float64)[:, None] * omega[None, :]
    return np.concatenate([np.sin(ang), np.cos(ang)], axis=-1)


def _grid_pos_embed(n_tokens, dim):
    t = np.arange(n_tokens)
    return np.concatenate([_sincos(t // GRID_W, dim // 2), _sincos(t % GRID_W, dim // 2)], axis=-1)


def kernel(x_prompt, x_sample, state_ssd_ctx, c, c_ctx, w_ada, b_ada, ln_g, ln_b, fno_w_in, fno_w_out,
           ssd_w_in, ssd_conv_w, ssd_conv_b, ssd_dt_bias, ssd_a_log, ssd_d, ssd_norm_w, ssd_w_out):
    batch, seq, d = x_prompt.shape
    dec_batch, dec_seq, _ = x_sample.shape
    ssd_width = ssd_w_out.shape[1]
    conv_dim = ssd_conv_w.shape[2]
    assert dec_batch <= CTX_ROW

    cond = jnp.zeros((COND_ROWS, d), F32).at[:dec_batch].set(c).at[CTX_ROW].set(c_ctx)
    mod = _modulation(cond, w_ada, b_ada)
    mod5 = mod.reshape(DEPTH, COND_ROWS, 3, 1, d)

    assert ssd_w_in.shape[2] == ssd_width + conv_dim + LANES
    wts = (ln_g, ln_b,
           fno_w_in[0].astype(BF16), fno_w_out[0].astype(BF16), ssd_w_in[0].astype(BF16),
           ssd_conv_w[0], ssd_conv_b[0], ssd_dt_bias[0], ssd_a_log[0],
           jnp.repeat(ssd_d[0], SSD_HEAD_DIM).reshape(1, ssd_width),
           ssd_norm_w[0], ssd_w_out[0].astype(BF16))

    y_prompt, fin = _trunk(x_prompt.reshape(batch * seq, d), None, batch, seq,
                           False, None, mod5, wts, want_final=True)
    pos = jnp.asarray(_grid_pos_embed(dec_seq, d), dtype=F32)
    y_sample, _ = _trunk(x_sample.reshape(dec_batch * dec_seq, d), pos, dec_batch, dec_seq,
                         True, state_ssd_ctx[:, 0], mod5, wts, want_final=False)
    return (y_prompt.reshape(batch, seq, d), y_sample.reshape(dec_batch, dec_seq, d),
            fin[:, None])
```

```python
import functools
import math

import numpy as np
import jax
import jax.numpy as jnp
from jax import lax
from jax.experimental import pallas as pl
from jax.experimental.pallas import tpu as pltpu

F32 = jnp.float32
BF16 = jnp.bfloat16

DEPTH = 2
GRID_W = 64
FNO_GROUP_DIM = 256
SSD_HEAD_DIM = 64
SSD_GROUPS = 8
SSD_D_STATE = 128
SSD_CONV_W = 5
CHUNK = 128
DEEPNORM_ALPHA = (2 * DEPTH) ** 0.25
LN_EPS = 1e-5
COND_ROWS = 16
CTX_ROW = 8
LANES = 128
SUBLANES = 8
MIB = 1024 * 1024


def _params(semantics, vmem_mib):
    return pltpu.CompilerParams(dimension_semantics=semantics, vmem_limit_bytes=vmem_mib * MIB)


def _silu(x):
    h = 0.5 * x
    return h + h * jnp.tanh(h)


def _dot(a, b):
    return jnp.dot(a, b, preferred_element_type=F32)


def _mod_kernel(c_ref, w_ref, b_ref, o_ref):
    s = _silu(c_ref[...]).astype(BF16)
    o_ref[...] = _dot(s, w_ref[...].astype(BF16)) + b_ref[...]


def _modulation(cond, w_ada, b_ada):
    depth, d, n = w_ada.shape
    tn = 1536
    return pl.pallas_call(
        _mod_kernel,
        grid=(depth, n // tn),
        in_specs=[
            pl.BlockSpec((COND_ROWS, d), lambda i, j: (0, 0)),
            pl.BlockSpec((None, d, tn), lambda i, j: (i, 0, j)),
            pl.BlockSpec((None, 1, tn), lambda i, j: (i, 0, j)),
        ],
        out_specs=pl.BlockSpec((None, COND_ROWS, tn), lambda i, j: (i, 0, j)),
        out_shape=jax.ShapeDtypeStruct((depth, COND_ROWS, n), F32),
        compiler_params=_params(("parallel", "parallel"), 40),
        name="adaln_mod",
    )(cond, w_ada, b_ada.reshape(depth, 1, n))


LN_ROWS = 256


def _ln_modulate(x, shift, scale):
    mu = jnp.mean(x, axis=-1, keepdims=True)
    xc = x - mu
    var = jnp.mean(xc * xc, axis=-1, keepdims=True)
    return xc * lax.rsqrt(var + LN_EPS) * (1.0 + scale) + shift


def _ln_mm_kernel(*refs, has_pos, has_dt):
    refs = list(refs)
    x_ref = refs.pop(0)
    pos_ref = refs.pop(0) if has_pos else None
    shift_ref, scale_ref, w_ref = refs.pop(0), refs.pop(0), refs.pop(0)
    wdt_ref = refs.pop(0) if has_dt else None
    o_ref = refs.pop(0)
    dt_ref = refs.pop(0) if has_dt else None
    h_scr = refs.pop(0)

    @pl.when(pl.program_id(1) == 0)
    def _():
        piece = min(LN_ROWS, x_ref.shape[0])
        for r in range(0, x_ref.shape[0], piece):
            rows = slice(r, r + piece)
            x = x_ref[rows, :]
            if has_pos:
                x = x + pos_ref[rows, :]
            h = _ln_modulate(x, shift_ref[...], scale_ref[...]).astype(BF16)
            h_scr[rows, :] = h
            if has_dt:
                dt_ref[rows, :] = _dot(h, wdt_ref[...])

    o_ref[...] = _dot(h_scr[...], w_ref[...]).astype(o_ref.dtype)


def _ln_in_proj(x, pos, mod5, layer, cond_row, w, with_dt, *, tm, tn):
    t, d = x.shape
    has_pos, has_dt = pos is not None, with_dt
    n = w.shape[1] - (LANES if has_dt else 0)
    in_specs = [pl.BlockSpec((tm, d), lambda i, j: (i, 0))]
    args = [x]
    if has_pos:
        pos_blocks = pos.shape[0] // tm
        mode = dict(pipeline_mode=pl.Buffered(1)) if pos_blocks == 1 else {}
        in_specs.append(pl.BlockSpec((tm, d), lambda i, j: (i % pos_blocks, 0), **mode))
        args.append(pos)
    for part in (0, 1):
        in_specs.append(pl.BlockSpec((None, None, None, 1, d),
                                     lambda i, j, part=part: (layer, cond_row(i), part, 0, 0)))
        args.append(mod5)
    in_specs.append(pl.BlockSpec((d, tn), lambda i, j: (0, j)))
    args.append(w)
    out_specs = [pl.BlockSpec((tm, tn), lambda i, j: (i, j))]
    out_shape = [jax.ShapeDtypeStruct((t, n), BF16)]
    if has_dt:
        in_specs.append(pl.BlockSpec((d, LANES), lambda i, j: (0, n // LANES)))
        args.append(w)
        out_specs.append(pl.BlockSpec((tm, LANES), lambda i, j: (i, 0)))
        out_shape.append(jax.ShapeDtypeStruct((t, LANES), F32))
    out = pl.pallas_call(
        functools.partial(_ln_mm_kernel, has_pos=has_pos, has_dt=has_dt),
        grid=(t // tm, n // tn),
        in_specs=in_specs,
        out_specs=out_specs,
        out_shape=out_shape,
        scratch_shapes=[pltpu.VMEM((tm, d), BF16)],
        compiler_params=_params(("parallel", "arbitrary"), 56),
        name="ln_in_proj",
    )(*args)
    return out if has_dt else out[0]


DFT_TAIL = 16


def _dft_matrices(seq, group_dim):
    def cos_sin(n):
        k = np.arange(n, dtype=np.int64)
        ang = 2.0 * np.pi * ((k[:, None] * k[None, :]) % n).astype(np.float64) / n
        return np.cos(ang), np.sin(ang)
    half = seq // 2
    cl, sl = cos_sin(seq)
    cc, sc = cos_sin(group_dim)
    norm = 1.0 / math.sqrt(seq * group_dim)
    posm = np.zeros((2 * half + DFT_TAIL, seq))
    posm[:half] = cl[:half] * norm
    posm[half:2 * half] = sl[:half] * norm
    posm[2 * half] = cl[half] * norm
    flip = np.zeros((half, half))
    flip[0, 0] = 1.0
    flip[np.arange(1, half), half - np.arange(1, half)] = 1.0
    as_bf16 = lambda a: jnp.asarray(a, dtype=F32).astype(BF16)
    return as_bf16(posm), as_bf16(np.stack([cc, sc])), as_bf16(flip)


def _fno_kernel(u_ref, z_ref, posm_ref, chan_ref, flip_ref, o_ref):
    seq, tw = u_ref.shape
    half = seq // 2
    cg = FNO_GROUP_DIM
    first_row = lax.broadcasted_iota(jnp.int32, (half, cg), 0) == 0
    groups = [slice(k * cg, (k + 1) * cg) for k in range(tw // cg)]
    ab = _dot(posm_ref[...], u_ref[...]).astype(BF16)
    p = [_dot(ab[:half, cols], chan_ref[0]) for cols in groups]
    q = [_dot(ab[half:2 * half, cols], chan_ref[1]) for cols in groups]
    p_nyq = [_dot(ab[2 * half:, cols], chan_ref[0])[0:1, :] for cols in groups]
    mirrored = jnp.concatenate(
        [jnp.where(first_row, p_nyq[k], p[k] + q[k]).astype(BF16) for k in range(len(groups))], axis=1)
    bottom = _dot(flip_ref[...], mirrored)
    for k, cols in enumerate(groups):
        gate = _silu(z_ref[:, cols].astype(F32))
        o_ref[:half, cols] = ((p[k] - q[k]) * gate[:half]).astype(o_ref.dtype)
        o_ref[half:, cols] = (bottom[:, cols] * gate[half:]).astype(o_ref.dtype)


def _fno_core(uz, batch, seq, width, *, tw):
    consts = _dft_matrices(seq, FNO_GROUP_DIM)
    nw = width // tw
    const_specs = [pl.BlockSpec(c.shape, lambda b, j, nd=c.ndim: (0,) * nd) for c in consts]
    return pl.pallas_call(
        _fno_kernel,
        grid=(batch, nw),
        in_specs=[
            pl.BlockSpec((seq, tw), lambda b, j: (b, j)),
            pl.BlockSpec((seq, tw), lambda b, j: (b, nw + j)),
        ] + const_specs,
        out_specs=pl.BlockSpec((seq, tw), lambda b, j: (b, j)),
        out_shape=jax.ShapeDtypeStruct((batch * seq, width), BF16),
        compiler_params=_params(("parallel", "parallel"), 48),
        name="fno_core",
    )(uz, uz, *consts)


def _out_ln_kernel(*refs, has_pos, gated):
    refs = list(refs)
    a_ref = refs.pop(0)
    z_ref, nw_ref = (refs.pop(0), refs.pop(0)) if gated else (None, None)
    w_ref, x_ref = refs.pop(0), refs.pop(0)
    pos_ref = refs.pop(0) if has_pos else None
    gate_ref, g_ref, b_ref, o_ref = refs

    if gated:
        y = a_ref[...].astype(F32) * _silu(z_ref[...].astype(F32))
        inv_rms = lax.rsqrt(jnp.mean(y * y, axis=-1, keepdims=True) + LN_EPS)
        acc = _dot((y * nw_ref[...]).astype(BF16), w_ref[...]) * inv_rms
    else:
        acc = _dot(a_ref[...], w_ref[...])
    x = x_ref[...]
    if has_pos:
        x = x + pos_ref[...]
    r = DEEPNORM_ALPHA * x + gate_ref[...] * acc
    mu = jnp.mean(r, axis=-1, keepdims=True)
    rc = r - mu
    var = jnp.mean(rc * rc, axis=-1, keepdims=True)
    o_ref[...] = rc * lax.rsqrt(var + LN_EPS) * g_ref[...] + b_ref[...]


def _out_proj_ln(a, z, norm_w, w, x, pos, mod5, layer, cond_row, ln_g, ln_b, *, tm):
    t, kdim = a.shape
    d = w.shape[1]
    has_pos, gated = pos is not None, z is not None
    row_k = pl.BlockSpec((tm, kdim), lambda i: (i, 0))
    row_d = pl.BlockSpec((tm, d), lambda i: (i, 0))
    in_specs, args = [row_k], [a]
    if gated:
        in_specs += [row_k, pl.BlockSpec((1, kdim), lambda i: (0, 0))]
        args += [z, norm_w.reshape(1, kdim)]
    in_specs += [pl.BlockSpec((kdim, d), lambda i: (0, 0), pipeline_mode=pl.Buffered(1)), row_d]
    args += [w, x]
    if has_pos:
        pos_blocks = pos.shape[0] // tm
        in_specs.append(pl.BlockSpec((tm, d), lambda i: (i % pos_blocks, 0)))
        args.append(pos)
    in_specs.append(pl.BlockSpec((None, None, None, 1, d), lambda i: (layer, cond_row(i), 2, 0, 0)))
    args.append(mod5)
    for v in (ln_g, ln_b):
        in_specs.append(pl.BlockSpec((None, 1, d), lambda i: (layer, 0, 0)))
        args.append(v.reshape(v.shape[0], 1, d))
    return pl.pallas_call(
        functools.partial(_out_ln_kernel, has_pos=has_pos, gated=gated),
        grid=(t // tm,),
        in_specs=in_specs,
        out_specs=row_d,
        out_shape=jax.ShapeDtypeStruct((t, d), F32),
        compiler_params=_params(("parallel",), 58),
        name="out_proj_ln",
    )(*args)


CONV_OFFSETS = (-2, -1, 1, 2)
CONV_PAD = 16
CONV_STRIP = 256
CONV_TILE = 64
CONV_BLOCK_ROWS = 1024


def _shift_matrix():
    tile = CONV_TILE
    s = np.zeros((len(CONV_OFFSETS) * tile, tile + 2 * CONV_PAD), np.float32)
    for t, off in enumerate(CONV_OFFSETS):
        s[t * tile + np.arange(tile), CONV_PAD + np.arange(tile) + off] = 1.0
    return jnp.asarray(s, dtype=BF16)


def _conv_kernel(x_ref, s_ref, w_ref, b_ref, o_ref, pad_scr, *, seq):
    tc = x_ref.shape[1]
    tile = CONV_TILE
    padded = seq + 2 * CONV_PAD
    centre = SSD_CONV_W // 2

    def rows_of(ref, k, lanes):
        return jnp.tile(ref[k * SUBLANES:(k + 1) * SUBLANES, lanes], (tile // SUBLANES, 1))

    for s in range(x_ref.shape[0] // seq):
        base = s * padded
        pad_scr[base:base + CONV_PAD, :] = jnp.zeros((CONV_PAD, tc), BF16)
        pad_scr[base + CONV_PAD + seq:base + padded, :] = jnp.zeros((CONV_PAD, tc), BF16)
        pad_scr[base + CONV_PAD:base + CONV_PAD + seq, :] = x_ref[s * seq:(s + 1) * seq, :]
        for i in range(seq // tile):
            for c in range(0, tc, CONV_STRIP):
                lanes = slice(c, c + CONV_STRIP)
                window = pad_scr[base + i * tile:base + (i + 1) * tile + 2 * CONV_PAD, lanes]
                taps = _dot(s_ref[...], window)
                acc = (rows_of(b_ref, 0, lanes)
                       + window[CONV_PAD:CONV_PAD + tile, :].astype(F32) * rows_of(w_ref, centre, lanes))
                for t, off in enumerate(CONV_OFFSETS):
                    acc = acc + taps[t * tile:(t + 1) * tile, :] * rows_of(w_ref, centre + off, lanes)
                o_ref[s * seq + i * tile:s * seq + (i + 1) * tile, lanes] = _silu(acc).astype(o_ref.dtype)


def _conv_silu(zx, conv_w, conv_b, batch, seq, col0, *, tc):
    conv_dim = conv_w.shape[1]
    assert col0 % tc == 0 and conv_dim % tc == 0
    c0 = col0 // tc
    w_rows = jnp.repeat(conv_w, SUBLANES, axis=0)
    b_rows = jnp.broadcast_to(conv_b.reshape(1, conv_dim), (SUBLANES, conv_dim))
    shift = _shift_matrix()
    nb = max(1, CONV_BLOCK_ROWS // seq)
    assert batch % nb == 0
    rows = nb * seq
    return pl.pallas_call(
        functools.partial(_conv_kernel, seq=seq),
        grid=(batch // nb, conv_dim // tc),
        in_specs=[pl.BlockSpec((rows, tc), lambda b, j: (b, c0 + j)),
                  pl.BlockSpec(shift.shape, lambda b, j: (0, 0)),
                  pl.BlockSpec((SSD_CONV_W * SUBLANES, tc), lambda b, j: (0, j)),
                  pl.BlockSpec((SUBLANES, tc), lambda b, j: (0, j))],
        out_specs=pl.BlockSpec((rows, tc), lambda b, j: (b, j)),
        out_shape=jax.ShapeDtypeStruct((batch * seq, conv_dim), BF16),
        scratch_shapes=[pltpu.VMEM((nb * (seq + 2 * CONV_PAD), tc), BF16)],
        compiler_params=_params(("parallel", "parallel"), 48),
        name="ssd_conv_silu",
    )(zx, shift, w_rows, b_rows)


def _split3(x):
    hi = x.astype(BF16)
    r = x - hi.astype(F32)
    mid = r.astype(BF16)
    lo = (r - mid.astype(F32)).astype(BF16)
    return hi, mid, lo


HEADS_PER_GROUP = 8
PAIRS = HEADS_PER_GROUP // 2
GROUP_W = HEADS_PER_GROUP * SSD_HEAD_DIM
N_DH = 2 * HEADS_PER_GROUP
SPLIT = 3
LOG2E = 1.4426950408889634


def _group_select_matrix():
    sel = np.zeros((SPLIT * LANES, SSD_GROUPS * LANES), np.float32)
    half = LANES // 2
    for g in range(SSD_GROUPS):
        for j in range(N_DH):
            d, h = divmod(j, HEADS_PER_GROUP)
            src = d * half + g * HEADS_PER_GROUP + h
            for p in range(SPLIT):
                sel[p * LANES + src, g * LANES + SPLIT * j + p] = 1.0
    return jnp.asarray(sel, dtype=BF16)


def _lane_bcast_matrices():
    e = np.zeros((2, LANES, HEADS_PER_GROUP * LANES), np.float32)
    for d in range(2):
        for h in range(HEADS_PER_GROUP):
            j = d * HEADS_PER_GROUP + h
            e[d, SPLIT * j:SPLIT * (j + 1), h * LANES:(h + 1) * LANES] = 1.0
    return jnp.asarray(e, dtype=BF16)


def _dt_kernel(raw_ref, bias_ref, alog_ref, sel_ref, col_ref, row_ref):
    n_ch = row_ref.shape[1]
    half = LANES // 2
    row = lax.broadcasted_iota(jnp.int32, (CHUNK, CHUNK), 0)
    col = lax.broadcasted_iota(jnp.int32, (CHUNK, CHUNK), 1)
    tri_f = jnp.where(row >= col, 1.0, 0.0).astype(BF16)
    tri_b = jnp.where(row <= col, 1.0, 0.0).astype(BF16)
    is_fwd = col < half
    a2 = -jnp.exp(alog_ref[...]) * LOG2E
    for k in range(n_ch):
        rows = slice(k * CHUNK, (k + 1) * CHUNK)
        x = raw_ref[rows, :] + bias_ref[...]
        dt = jnp.maximum(x, 0.0) + jnp.log1p(jnp.exp(-jnp.abs(x)))
        parts = _split3(dt * a2)
        cs = jnp.where(is_fwd, sum(_dot(tri_f, p) for p in parts), sum(_dot(tri_b, p) for p in parts))
        tot = jnp.where(is_fwd[:1], cs[CHUNK - 1:CHUNK, :], cs[0:1, :])
        w = dt * jnp.exp2(tot - cs)
        pieces = jnp.concatenate(_split3(cs), axis=1)
        grouped = _dot(pieces, sel_ref[...]).astype(BF16)
        log_dt = jnp.log2(dt)
        transposed = ((cs - log_dt).T, log_dt.T, w.T)
        for g in range(SSD_GROUPS):
            col_ref[g, rows, :] = grouped[:, g * LANES:(g + 1) * LANES]
            for n, v in enumerate(transposed):
                for d in (0, 1):
                    src = d * half + g * HEADS_PER_GROUP
                    dst = n * N_DH + d * HEADS_PER_GROUP
                    row_ref[g, k, dst:dst + HEADS_PER_GROUP, :] = v[src:src + HEADS_PER_GROUP, :]


def _dt_prep(dt_raw, dt_bias, a_log, *, n_ch):
    t = dt_raw.shape[0]
    tm = n_ch * CHUNK
    sel = _group_select_matrix()
    vec_spec = pl.BlockSpec((1, LANES), lambda i: (0, 0))
    return pl.pallas_call(
        _dt_kernel,
        grid=(t // tm,),
        in_specs=[pl.BlockSpec((tm, LANES), lambda i: (i, 0)), vec_spec, vec_spec,
                  pl.BlockSpec(sel.shape, lambda i: (0, 0))],
        out_specs=[pl.BlockSpec((SSD_GROUPS, tm, LANES), lambda i: (0, i, 0)),
                   pl.BlockSpec((SSD_GROUPS, n_ch, 3 * N_DH, CHUNK), lambda i: (0, i, 0, 0))],
        out_shape=[jax.ShapeDtypeStruct((SSD_GROUPS, t, LANES), BF16),
                   jax.ShapeDtypeStruct((SSD_GROUPS, t // CHUNK, 3 * N_DH, CHUNK), F32)],
        compiler_params=_params(("parallel",), 32),
        name="ssd_dt_prep",
    )(dt_raw, dt_bias.reshape(1, LANES), a_log.reshape(1, LANES), sel)


SSD_SEQS_PER_STEP = 2


def _ssd_kernel(*refs, nc, nb, has_init, want_final):
    refs = list(refs)
    xs_ref, b_ref, c_ref, col_ref, row_ref, bcast_ref, dsk_ref = refs[:7]
    del refs[:7]
    init_ref = refs.pop(0) if has_init else None
    y_ref = refs.pop(0)
    fin_ref = refs.pop(0) if want_final else None
    st_scr, y_scr = refs

    p = SSD_HEAD_DIM
    row = lax.broadcasted_iota(jnp.int32, (CHUNK, CHUNK), 0)
    col = lax.broadcasted_iota(jnp.int32, (CHUNK, CHUNK), 1)
    lo_half = col < p

    if has_init:
        for sb in range(nb):
            for d in (0, 1):
                for q in range(PAIRS):
                    blk = init_ref[sb, d, 2 * q:2 * q + 2].reshape(2 * p, SSD_D_STATE)
                    st_scr[sb, d, :, q * LANES:(q + 1) * LANES] = blk.T

    def chunk_body(i, carry, first, zero_state=False):
        for sb, d in [(sb, d) for sb in range(nb) for d in (0, 1)]:
            causal = (row >= col) if d == 0 else (row <= col)
            c = sb * nc + (i if d == 0 else nc - 1 - i)
            r0 = c * CHUNK if isinstance(c, int) else pl.multiple_of(c * CHUNK, CHUNK)
            rows = pl.ds(r0, CHUNK)
            bc = b_ref[rows, :]
            cc = c_ref[rows, :]
            g = lax.dot_general(cc, bc, (((1,), (1,)), ((), ())), preferred_element_type=F32)
            gm = jnp.where(causal, g, 0.0)
            bt = bc.astype(F32).T
            cols = col_ref[rows, :]
            edge = CHUNK - 1 if d == 0 else 0
            if not zero_state:
                y_off = _dot(cc, st_scr[sb, d].astype(BF16))
            spread = _dot(cols, bcast_ref[d])
            xbd_q, m_q, bt_q, e_q = [], [], [], []
            for q in range(PAIRS):
                x = xs_ref[rows, q * LANES:(q + 1) * LANES]
                zero = jnp.zeros_like(x)
                xbd_q.append(jnp.concatenate([jnp.where(lo_half, x, zero), jnp.where(lo_half, zero, x)], axis=0))
                m_h, bt_h, e_h = [], [], []
                for h in (2 * q, 2 * q + 1):
                    j = d * HEADS_PER_GROUP + h
                    cs_l = spread[:, h * LANES:(h + 1) * LANES]
                    csd_s = row_ref[c, pl.ds(j, 1), :]
                    ldt_s = row_ref[c, pl.ds(N_DH + j, 1), :]
                    w_s = row_ref[c, pl.ds(2 * N_DH + j, 1), :]
                    lmat_dt = jnp.exp2(jnp.minimum(cs_l - csd_s, ldt_s))
                    m_h.append((gm * lmat_dt).astype(BF16))
                    bt_h.append((bt * w_s).astype(BF16))
                    e_h.append(cs_l)
                m_q.append(jnp.concatenate(m_h, axis=1))
                bt_q.append(jnp.concatenate(bt_h, axis=1))
                if not zero_state:
                    e_q.append(jnp.exp2(jnp.where(lo_half, e_h[0], e_h[1])))
            y_diag = [_dot(m_q[q], xbd_q[q]) for q in range(PAIRS)]
            s_new = [_dot(bt_q[q], xbd_q[q]) for q in range(PAIRS)]
            for q in range(PAIRS):
                lanes = slice(q * LANES, (q + 1) * LANES)
                y = y_diag[q] if zero_state else y_diag[q] + y_off[:, lanes] * e_q[q]
                if first:
                    y_scr[rows, lanes] = y + dsk_ref[:, lanes] * xs_ref[rows, lanes].astype(F32)
                else:
                    y_ref[rows, lanes] = (y_scr[rows, lanes] + y).astype(y_ref.dtype)
                if zero_state:
                    st_scr[sb, d, :, lanes] = s_new[q]
                else:
                    st_scr[sb, d, :, lanes] = (st_scr[sb, d, :, lanes] * e_q[q][edge:edge + 1, :]
                                               + s_new[q])
        return carry

    assert nc % 2 == 0
    start = 0
    if not has_init:
        chunk_body(0, 0, first=True, zero_state=True)
        start = 1
    lax.fori_loop(start, nc // 2, functools.partial(chunk_body, first=True), 0)
    lax.fori_loop(nc // 2, nc, functools.partial(chunk_body, first=False), 0)

    if want_final:
        for sb in range(nb):
            for d in (0, 1):
                for q in range(PAIRS):
                    lanes = slice(q * LANES, (q + 1) * LANES)
                    fin_ref[sb, d, 2 * q:2 * q + 2] = st_scr[sb, d, :, lanes].T.reshape(2, p, SSD_D_STATE)


def _ssd_scan(xbc, cols_g, rows_g, d_skip, init, batch, seq, *, want_final):
    nc = seq // CHUNK
    width = SSD_GROUPS * GROUP_W
    heads = SSD_GROUPS * HEADS_PER_GROUP
    has_init = init is not None
    bcast = _lane_bcast_matrices()
    nb = SSD_SEQS_PER_STEP * (2 if nc <= 2 else 1)
    assert batch % nb == 0
    rows = nb * seq
    in_specs = [
        pl.BlockSpec((rows, GROUP_W), lambda b, g: (b, g)),
        pl.BlockSpec((rows, SSD_D_STATE), lambda b, g: (b, width // SSD_D_STATE + g)),
        pl.BlockSpec((rows, SSD_D_STATE), lambda b, g: (b, width // SSD_D_STATE + SSD_GROUPS + g)),
        pl.BlockSpec((None, rows, LANES), lambda b, g: (g, b, 0)),
        pl.BlockSpec((None, nb * nc, 3 * N_DH, CHUNK), lambda b, g: (g, b, 0, 0)),
        pl.BlockSpec(bcast.shape, lambda b, g: (0, 0, 0)),
        pl.BlockSpec((1, GROUP_W), lambda b, g: (0, g)),
    ]
    args = [xbc, xbc, xbc, cols_g, rows_g, bcast, d_skip]
    state_spec = pl.BlockSpec((nb, 2, HEADS_PER_GROUP, SSD_HEAD_DIM, SSD_D_STATE),
                              lambda b, g: (b, 0, g, 0, 0))
    if has_init:
        in_specs.append(state_spec)
        args.append(init)
    out_specs = [pl.BlockSpec((rows, GROUP_W), lambda b, g: (b, g))]
    out_shape = [jax.ShapeDtypeStruct((batch * seq, width), BF16)]
    if want_final:
        out_specs.append(state_spec)
        out_shape.append(jax.ShapeDtypeStruct((batch, 2, heads, SSD_HEAD_DIM, SSD_D_STATE), F32))
    out = pl.pallas_call(
        functools.partial(_ssd_kernel, nc=nc, nb=nb, has_init=has_init, want_final=want_final),
        grid=(batch // nb, SSD_GROUPS),
        in_specs=in_specs,
        out_specs=out_specs,
        out_shape=out_shape,
        scratch_shapes=[pltpu.VMEM((nb, 2, SSD_D_STATE, GROUP_W), F32), pltpu.VMEM((rows, GROUP_W), F32)],
        compiler_params=_params(("parallel", "parallel"), 48),
        name="ssd_scan",
    )(*args)
    return out if want_final else (out[0], None)


def _trunk(x, pos, batch, seq, per_seq_cond, init, mod5, wts, *, want_final):
    (ln_g, ln_b, fno_w_in, fno_w_out, ssd_w_in, conv_w, conv_b, dt_bias, a_log,
     d_skip, norm_w, ssd_w_out) = wts
    fno_width = fno_w_out.shape[0]
    ssd_width = ssd_w_out.shape[0]
    tm_in, tm_out, tm_gated = 1024, 512, 512

    def cond_row(tm):
        if not per_seq_cond:
            return lambda i: CTX_ROW
        assert seq % tm == 0
        return lambda i: (i * tm) // seq

    tn0 = 2048 if pos is None else 1024
    uz = _ln_in_proj(x, pos, mod5, 0, cond_row(tm_in), fno_w_in, False, tm=tm_in, tn=tn0)
    yg = _fno_core(uz, batch, seq, fno_width, tw=min(fno_width, 4096 * 256 // seq))
    x1 = _out_proj_ln(yg, None, None, fno_w_out, x, pos, mod5, 0, cond_row(tm_out), ln_g, ln_b, tm=tm_out)

    zx, dt_raw = _ln_in_proj(x1, None, mod5, 1, cond_row(tm_in), ssd_w_in, True, tm=tm_in, tn=2048)
    xbc = _conv_silu(zx, conv_w, conv_b, batch, seq, ssd_width, tc=2048)
    cols_g, rows_g = _dt_prep(dt_raw, dt_bias, a_log, n_ch=16)
    y, fin = _ssd_scan(xbc, cols_g, rows_g, d_skip, init, batch, seq, want_final=want_final)
    x2 = _out_proj_ln(y, zx, norm_w, ssd_w_out, x1, None, mod5, 1, cond_row(tm_gated), ln_g, ln_b, tm=tm_gated)
    return x2, fin


def _sincos(pos, dim):
    omega = 1.0 / (10000.0 ** (np.arange(dim // 2, dtype=np.float64) / (dim / 2)))
    ang = pos.astype(np.float64)[:, None] * omega[None, :]
    return np.concatenate([np.sin(ang), np.cos(ang)], axis=-1)


def _grid_pos_embed(n_tokens, dim):
    t = np.arange(n_tokens)
    return np.concatenate([_sincos(t // GRID_W, dim // 2), _sincos(t % GRID_W, dim // 2)], axis=-1)


def kernel(x_prompt, x_sample, state_ssd_ctx, c, c_ctx, w_ada, b_ada, ln_g, ln_b, fno_w_in, fno_w_out,
           ssd_w_in, ssd_conv_w, ssd_conv_b, ssd_dt_bias, ssd_a_log, ssd_d, ssd_norm_w, ssd_w_out):
    batch, seq, d = x_prompt.shape
    dec_batch, dec_seq, _ = x_sample.shape
    ssd_width = ssd_w_out.shape[1]
    conv_dim = ssd_conv_w.shape[2]
    assert dec_batch <= CTX_ROW

    cond = jnp.zeros((COND_ROWS, d), F32).at[:dec_batch].set(c).at[CTX_ROW].set(c_ctx)
    mod = _modulation(cond, w_ada, b_ada)
    mod5 = mod.reshape(DEPTH, COND_ROWS, 3, 1, d)

    assert ssd_w_in.shape[2] == ssd_width + conv_dim + LANES
    wts = (ln_g, ln_b,
           fno_w_in[0].astype(BF16), fno_w_out[0].astype(BF16), ssd_w_in[0].astype(BF16),
           ssd_conv_w[0], ssd_conv_b[0], ssd_dt_bias[0], ssd_a_log[0],
           jnp.repeat(ssd_d[0], SSD_HEAD_DIM).reshape(1, ssd_width),
           ssd_norm_w[0], ssd_w_out[0].astype(BF16))

    y_prompt, fin = _trunk(x_prompt.reshape(batch * seq, d), None, batch, seq,
                           False, None, mod5, wts, want_final=True)
    pos = jnp.asarray(_grid_pos_embed(dec_seq, d), dtype=F32)
    y_sample, _ = _trunk(x_sample.reshape(dec_batch * dec_seq, d), pos, dec_batch, dec_seq,
                         True, state_ssd_ctx[:, 0], mod5, wts, want_final=False)
    return (y_prompt.reshape(batch, seq, d), y_sample.reshape(dec_batch, dec_seq, d),
            fin[:, None])
```

```python
import functools
import math

import numpy as np
import jax
import jax.numpy as jnp
from jax import lax
from jax.experimental import pallas as pl
from jax.experimental.pallas import tpu as pltpu

F32 = jnp.float32
BF16 = jnp.bfloat16

DEPTH = 2
GRID_W = 64
FNO_GROUP_DIM = 256
SSD_HEAD_DIM = 64
SSD_GROUPS = 8
SSD_D_STATE = 128
SSD_CONV_W = 5
CHUNK = 128
DEEPNORM_ALPHA = (2 * DEPTH) ** 0.25
LN_EPS = 1e-5
COND_ROWS = 16
CTX_ROW = 8
LANES = 128
SUBLANES = 8
MIB = 1024 * 1024


def _params(semantics, vmem_mib):
    return pltpu.CompilerParams(dimension_semantics=semantics, vmem_limit_bytes=vmem_mib * MIB)


def _silu(x):
    h = 0.5 * x
    return h + h * jnp.tanh(h)


def _dot(a, b):
    return jnp.dot(a, b, preferred_element_type=F32)


def _mod_kernel(c_ref, w_ref, b_ref, o_ref):
    s = _silu(c_ref[...]).astype(BF16)
    o_ref[...] = _dot(s, w_ref[...].astype(BF16)) + b_ref[...]


def _modulation(cond, w_ada, b_ada):
    depth, d, n = w_ada.shape
    tn = 1536
    return pl.pallas_call(
        _mod_kernel,
        grid=(depth, n // tn),
        in_specs=[
            pl.BlockSpec((COND_ROWS, d), lambda i, j: (0, 0)),
            pl.BlockSpec((None, d, tn), lambda i, j: (i, 0, j)),
            pl.BlockSpec((None, 1, tn), lambda i, j: (i, 0, j)),
        ],
        out_specs=pl.BlockSpec((None, COND_ROWS, tn), lambda i, j: (i, 0, j)),
        out_shape=jax.ShapeDtypeStruct((depth, COND_ROWS, n), F32),
        compiler_params=_params(("parallel", "parallel"), 40),
        name="adaln_mod",
    )(cond, w_ada, b_ada.reshape(depth, 1, n))


LN_ROWS = 256


def _ln_modulate(x, shift, scale):
    mu = jnp.mean(x, axis=-1, keepdims=True)
    xc = x - mu
    var = jnp.mean(xc * xc, axis=-1, keepdims=True)
    return xc * lax.rsqrt(var + LN_EPS) * (1.0 + scale) + shift


def _ln_mm_kernel(*refs, has_pos, has_dt):
    refs = list(refs)
    x_ref = refs.pop(0)
    pos_ref = refs.pop(0) if has_pos else None
    shift_ref, scale_ref, w_ref = refs.pop(0), refs.pop(0), refs.pop(0)
    wdt_ref = refs.pop(0) if has_dt else None
    o_ref = refs.pop(0)
    dt_ref = refs.pop(0) if has_dt else None
    h_scr = refs.pop(0)

    @pl.when(pl.program_id(1) == 0)
    def _():
        piece = min(LN_ROWS, x_ref.shape[0])
        for r in range(0, x_ref.shape[0], piece):
            rows = slice(r, r + piece)
            x = x_ref[rows, :]
            if has_pos:
                x = x + pos_ref[rows, :]
            h = _ln_modulate(x, shift_ref[...], scale_ref[...]).astype(BF16)
            h_scr[rows, :] = h
            if has_dt:
                dt_ref[rows, :] = _dot(h, wdt_ref[...])

    o_ref[...] = _dot(h_scr[...], w_ref[...]).astype(o_ref.dtype)


def _ln_in_proj(x, pos, mod5, layer, cond_row, w, with_dt, *, tm, tn):
    t, d = x.shape
    has_pos, has_dt = pos is not None, with_dt
    n = w.shape[1] - (LANES if has_dt else 0)
    in_specs = [pl.BlockSpec((tm, d), lambda i, j: (i, 0))]
    args = [x]
    if has_pos:
        pos_blocks = pos.shape[0] // tm
        mode = dict(pipeline_mode=pl.Buffered(1)) if pos_blocks == 1 else {}
        in_specs.append(pl.BlockSpec((tm, d), lambda i, j: (i % pos_blocks, 0), **mode))
        args.append(pos)
    for part in (0, 1):
        in_specs.append(pl.BlockSpec((None, None, None, 1, d),
                                     lambda i, j, part=part: (layer, cond_row(i), part, 0, 0)))
        args.append(mod5)
    in_specs.append(pl.BlockSpec((d, tn), lambda i, j: (0, j)))
    args.append(w)
    out_specs = [pl.BlockSpec((tm, tn), lambda i, j: (i, j))]
    out_shape = [jax.ShapeDtypeStruct((t, n), BF16)]
    if has_dt:
        in_specs.append(pl.BlockSpec((d, LANES), lambda i, j: (0, n // LANES)))
        args.append(w)
        out_specs.append(pl.BlockSpec((tm, LANES), lambda i, j: (i, 0)))
        out_shape.append(jax.ShapeDtypeStruct((t, LANES), F32))
    out = pl.pallas_call(
        functools.partial(_ln_mm_kernel, has_pos=has_pos, has_dt=has_dt),
        grid=(t // tm, n // tn),
        in_specs=in_specs,
        out_specs=out_specs,
        out_shape=out_shape,
        scratch_shapes=[pltpu.VMEM((tm, d), BF16)],
        compiler_params=_params(("parallel", "arbitrary"), 56),
        name="ln_in_proj",
    )(*args)
    return out if has_dt else out[0]


DFT_TAIL = 16


def _dft_matrices(seq, group_dim):
    def cos_sin(n):
        k = np.arange(n, dtype=np.int64)
        ang = 2.0 * np.pi * ((k[:, None] * k[None, :]) % n).astype(np.float64) / n
        return np.cos(ang), np.sin(ang)
    half = seq // 2
    cl, sl = cos_sin(seq)
    cc, sc = cos_sin(group_dim)
    norm = 1.0 / math.sqrt(seq * group_dim)
    posm = np.zeros((2 * half + DFT_TAIL, seq))
    posm[:half] = cl[:half] * norm
    posm[half:2 * half] = sl[:half] * norm
    posm[2 * half] = cl[half] * norm
    flip = np.zeros((half, half))
    flip[0, 0] = 1.0
    flip[np.arange(1, half), half - np.arange(1, half)] = 1.0
    as_bf16 = lambda a: jnp.asarray(a, dtype=F32).astype(BF16)
    return as_bf16(posm), as_bf16(np.stack([cc, sc])), as_bf16(flip)


def _fno_kernel(u_ref, z_ref, posm_ref, chan_ref, flip_ref, o_ref):
    seq, tw = u_ref.shape
    half = seq // 2
    cg = FNO_GROUP_DIM
    first_row = lax.broadcasted_iota(jnp.int32, (half, cg), 0) == 0
    groups = [slice(k * cg, (k + 1) * cg) for k in range(tw // cg)]
    ab = _dot(posm_ref[...], u_ref[...]).astype(BF16)
    p = [_dot(ab[:half, cols], chan_ref[0]) for cols in groups]
    q = [_dot(ab[half:2 * half, cols], chan_ref[1]) for cols in groups]
    p_nyq = [_dot(ab[2 * half:, cols], chan_ref[0])[0:1, :] for cols in groups]
    mirrored = jnp.concatenate(
        [jnp.where(first_row, p_nyq[k], p[k] + q[k]).astype(BF16) for k in range(len(groups))], axis=1)
    bottom = _dot(flip_ref[...], mirrored)
    for k, cols in enumerate(groups):
        gate = _silu(z_ref[:, cols].astype(F32))
        o_ref[:half, cols] = ((p[k] - q[k]) * gate[:half]).astype(o_ref.dtype)
        o_ref[half:, cols] = (bottom[:, cols] * gate[half:]).astype(o_ref.dtype)


def _fno_core(uz, batch, seq, width, *, tw):
    consts = _dft_matrices(seq, FNO_GROUP_DIM)
    nw = width // tw
    const_specs = [pl.BlockSpec(c.shape, lambda b, j, nd=c.ndim: (0,) * nd) for c in consts]
    return pl.pallas_call(
        _fno_kernel,
        grid=(batch, nw),
        in_specs=[
            pl.BlockSpec((seq, tw), lambda b, j: (b, j)),
            pl.BlockSpec((seq, tw), lambda b, j: (b, nw + j)),
        ] + const_specs,
        out_specs=pl.BlockSpec((seq, tw), lambda b, j: (b, j)),
        out_shape=jax.ShapeDtypeStruct((batch * seq, width), BF16),
        compiler_params=_params(("parallel", "parallel"), 58),
        name="fno_core",
    )(uz, uz, *consts)


def _out_ln_kernel(*refs, has_pos, gated):
    refs = list(refs)
    a_ref = refs.pop(0)
    z_ref, nw_ref = (refs.pop(0), refs.pop(0)) if gated else (None, None)
    w_ref, x_ref = refs.pop(0), refs.pop(0)
    pos_ref = refs.pop(0) if has_pos else None
    gate_ref, g_ref, b_ref, o_ref = refs

    if gated:
        y = a_ref[...].astype(F32) * _silu(z_ref[...].astype(F32))
        inv_rms = lax.rsqrt(jnp.mean(y * y, axis=-1, keepdims=True) + LN_EPS)
        acc = _dot((y * nw_ref[...]).astype(BF16), w_ref[...]) * inv_rms
    else:
        acc = _dot(a_ref[...], w_ref[...])
    x = x_ref[...]
    if has_pos:
        x = x + pos_ref[...]
    r = DEEPNORM_ALPHA * x + gate_ref[...] * acc
    mu = jnp.mean(r, axis=-1, keepdims=True)
    rc = r - mu
    var = jnp.mean(rc * rc, axis=-1, keepdims=True)
    o_ref[...] = rc * lax.rsqrt(var + LN_EPS) * g_ref[...] + b_ref[...]


def _out_proj_ln(a, z, norm_w, w, x, pos, mod5, layer, cond_row, ln_g, ln_b, *, tm):
    t, kdim = a.shape
    d = w.shape[1]
    has_pos, gated = pos is not None, z is not None
    row_k = pl.BlockSpec((tm, kdim), lambda i: (i, 0))
    row_d = pl.BlockSpec((tm, d), lambda i: (i, 0))
    in_specs, args = [row_k], [a]
    if gated:
        in_specs += [row_k, pl.BlockSpec((1, kdim), lambda i: (0, 0))]
        args += [z, norm_w.reshape(1, kdim)]
    in_specs += [pl.BlockSpec((kdim, d), lambda i: (0, 0), pipeline_mode=pl.Buffered(1)), row_d]
    args += [w, x]
    if has_pos:
        pos_blocks = pos.shape[0] // tm
        in_specs.append(pl.BlockSpec((tm, d), lambda i: (i % pos_blocks, 0)))
        args.append(pos)
    in_specs.append(pl.BlockSpec((None, None, None, 1, d), lambda i: (layer, cond_row(i), 2, 0, 0)))
    args.append(mod5)
    for v in (ln_g, ln_b):
        in_specs.append(pl.BlockSpec((None, 1, d), lambda i: (layer, 0, 0)))
        args.append(v.reshape(v.shape[0], 1, d))
    return pl.pallas_call(
        functools.partial(_out_ln_kernel, has_pos=has_pos, gated=gated),
        grid=(t // tm,),
        in_specs=in_specs,
        out_specs=row_d,
        out_shape=jax.ShapeDtypeStruct((t, d), F32),
        compiler_params=_params(("parallel",), 58),
        name="out_proj_ln",
    )(*args)


CONV_OFFSETS = (-2, -1, 1, 2)
CONV_PAD = 16
CONV_STRIP = 256
CONV_TILE = 64
CONV_BLOCK_ROWS = 1024


def _shift_matrix():
    tile = CONV_TILE
    s = np.zeros((len(CONV_OFFSETS) * tile, tile + 2 * CONV_PAD), np.float32)
    for t, off in enumerate(CONV_OFFSETS):
        s[t * tile + np.arange(tile), CONV_PAD + np.arange(tile) + off] = 1.0
    return jnp.asarray(s, dtype=BF16)


def _conv_kernel(x_ref, s_ref, w_ref, b_ref, o_ref, pad_scr, *, seq):
    tc = x_ref.shape[1]
    tile = CONV_TILE
    padded = seq + 2 * CONV_PAD
    centre = SSD_CONV_W // 2

    def rows_of(ref, k, lanes):
        return jnp.tile(ref[k * SUBLANES:(k + 1) * SUBLANES, lanes], (tile // SUBLANES, 1))

    for s in range(x_ref.shape[0] // seq):
        base = s * padded
        pad_scr[base:base + CONV_PAD, :] = jnp.zeros((CONV_PAD, tc), BF16)
        pad_scr[base + CONV_PAD + seq:base + padded, :] = jnp.zeros((CONV_PAD, tc), BF16)
        pad_scr[base + CONV_PAD:base + CONV_PAD + seq, :] = x_ref[s * seq:(s + 1) * seq, :]
        for i in range(seq // tile):
            for c in range(0, tc, CONV_STRIP):
                lanes = slice(c, c + CONV_STRIP)
                window = pad_scr[base + i * tile:base + (i + 1) * tile + 2 * CONV_PAD, lanes]
                taps = _dot(s_ref[...], window)
                acc = (rows_of(b_ref, 0, lanes)
                       + window[CONV_PAD:CONV_PAD + tile, :].astype(F32) * rows_of(w_ref, centre, lanes))
                for t, off in enumerate(CONV_OFFSETS):
                    acc = acc + taps[t * tile:(t + 1) * tile, :] * rows_of(w_ref, centre + off, lanes)
                o_ref[s * seq + i * tile:s * seq + (i + 1) * tile, lanes] = _silu(acc).astype(o_ref.dtype)


def _conv_silu(zx, conv_w, conv_b, batch, seq, col0, *, tc):
    conv_dim = conv_w.shape[1]
    assert col0 % tc == 0 and conv_dim % tc == 0
    c0 = col0 // tc
    w_rows = jnp.repeat(conv_w, SUBLANES, axis=0)
    b_rows = jnp.broadcast_to(conv_b.reshape(1, conv_dim), (SUBLANES, conv_dim))
    shift = _shift_matrix()
    nb = max(1, CONV_BLOCK_ROWS // seq)
    assert batch % nb == 0
    rows = nb * seq
    return pl.pallas_call(
        functools.partial(_conv_kernel, seq=seq),
        grid=(batch // nb, conv_dim // tc),
        in_specs=[pl.BlockSpec((rows, tc), lambda b, j: (b, c0 + j)),
                  pl.BlockSpec(shift.shape, lambda b, j: (0, 0)),
                  pl.BlockSpec((SSD_CONV_W * SUBLANES, tc), lambda b, j: (0, j)),
                  pl.BlockSpec((SUBLANES, tc), lambda b, j: (0, j))],
        out_specs=pl.BlockSpec((rows, tc), lambda b, j: (b, j)),
        out_shape=jax.ShapeDtypeStruct((batch * seq, conv_dim), BF16),
        scratch_shapes=[pltpu.VMEM((nb * (seq + 2 * CONV_PAD), tc), BF16)],
        compiler_params=_params(("parallel", "parallel"), 48),
        name="ssd_conv_silu",
    )(zx, shift, w_rows, b_rows)


def _split3(x):
    hi = x.astype(BF16)
    r = x - hi.astype(F32)
    mid = r.astype(BF16)
    lo = (r - mid.astype(F32)).astype(BF16)
    return hi, mid, lo


HEADS_PER_GROUP = 8
PAIRS = HEADS_PER_GROUP // 2
GROUP_W = HEADS_PER_GROUP * SSD_HEAD_DIM
N_DH = 2 * HEADS_PER_GROUP
SPLIT = 3
LOG2E = 1.4426950408889634


def _group_select_matrix():
    sel = np.zeros((SPLIT * LANES, SSD_GROUPS * LANES), np.float32)
    half = LANES // 2
    for g in range(SSD_GROUPS):
        for j in range(N_DH):
            d, h = divmod(j, HEADS_PER_GROUP)
            src = d * half + g * HEADS_PER_GROUP + h
            for p in range(SPLIT):
                sel[p * LANES + src, g * LANES + SPLIT * j + p] = 1.0
    return jnp.asarray(sel, dtype=BF16)


def _lane_bcast_matrices():
    e = np.zeros((2, LANES, HEADS_PER_GROUP * LANES), np.float32)
    for d in range(2):
        for h in range(HEADS_PER_GROUP):
            j = d * HEADS_PER_GROUP + h
            e[d, SPLIT * j:SPLIT * (j + 1), h * LANES:(h + 1) * LANES] = 1.0
    return jnp.asarray(e, dtype=BF16)


def _dt_kernel(raw_ref, bias_ref, alog_ref, sel_ref, col_ref, row_ref):
    n_ch = row_ref.shape[1]
    half = LANES // 2
    row = lax.broadcasted_iota(jnp.int32, (CHUNK, CHUNK), 0)
    col = lax.broadcasted_iota(jnp.int32, (CHUNK, CHUNK), 1)
    tri_f = jnp.where(row >= col, 1.0, 0.0).astype(BF16)
    tri_b = jnp.where(row <= col, 1.0, 0.0).astype(BF16)
    is_fwd = col < half
    a2 = -jnp.exp(alog_ref[...]) * LOG2E
    for k in range(n_ch):
        rows = slice(k * CHUNK, (k + 1) * CHUNK)
        x = raw_ref[rows, :] + bias_ref[...]
        dt = jnp.maximum(x, 0.0) + jnp.log1p(jnp.exp(-jnp.abs(x)))
        parts = _split3(dt * a2)
        cs = jnp.where(is_fwd, sum(_dot(tri_f, p) for p in parts), sum(_dot(tri_b, p) for p in parts))
        tot = jnp.where(is_fwd[:1], cs[CHUNK - 1:CHUNK, :], cs[0:1, :])
        w = dt * jnp.exp2(tot - cs)
        pieces = jnp.concatenate(_split3(cs), axis=1)
        grouped = _dot(pieces, sel_ref[...]).astype(BF16)
        log_dt = jnp.log2(dt)
        transposed = ((cs - log_dt).T, log_dt.T, w.T)
        for g in range(SSD_GROUPS):
            col_ref[g, rows, :] = grouped[:, g * LANES:(g + 1) * LANES]
            for n, v in enumerate(transposed):
                for d in (0, 1):
                    src = d * half + g * HEADS_PER_GROUP
                    dst = n * N_DH + d * HEADS_PER_GROUP
                    row_ref[g, k, dst:dst + HEADS_PER_GROUP, :] = v[src:src + HEADS_PER_GROUP, :]


def _dt_prep(dt_raw, dt_bias, a_log, *, n_ch):
    t = dt_raw.shape[0]
    tm = n_ch * CHUNK
    sel = _group_select_matrix()
    vec_spec = pl.BlockSpec((1, LANES), lambda i: (0, 0))
    return pl.pallas_call(
        _dt_kernel,
        grid=(t // tm,),
        in_specs=[pl.BlockSpec((tm, LANES), lambda i: (i, 0)), vec_spec, vec_spec,
                  pl.BlockSpec(sel.shape, lambda i: (0, 0))],
        out_specs=[pl.BlockSpec((SSD_GROUPS, tm, LANES), lambda i: (0, i, 0)),
                   pl.BlockSpec((SSD_GROUPS, n_ch, 3 * N_DH, CHUNK), lambda i: (0, i, 0, 0))],
        out_shape=[jax.ShapeDtypeStruct((SSD_GROUPS, t, LANES), BF16),
                   jax.ShapeDtypeStruct((SSD_GROUPS, t // CHUNK, 3 * N_DH, CHUNK), F32)],
        compiler_params=_params(("parallel",), 32),
        name="ssd_dt_prep",
    )(dt_raw, dt_bias.reshape(1, LANES), a_log.reshape(1, LANES), sel)


SSD_SEQS_PER_STEP = 4


def _ssd_kernel(*refs, nc, nb, has_init, want_final):
    refs = list(refs)
    xs_ref, b_ref, c_ref, col_ref, row_ref, bcast_ref, dsk_ref = refs[:7]
    del refs[:7]
    init_ref = refs.pop(0) if has_init else None
    y_ref = refs.pop(0)
    fin_ref = refs.pop(0) if want_final else None
    st_scr, y_scr = refs

    p = SSD_HEAD_DIM
    row = lax.broadcasted_iota(jnp.int32, (CHUNK, CHUNK), 0)
    col = lax.broadcasted_iota(jnp.int32, (CHUNK, CHUNK), 1)
    lo_half = col < p

    if has_init:
        for sb in range(nb):
            for d in (0, 1):
                for q in range(PAIRS):
                    blk = init_ref[sb, d, 2 * q:2 * q + 2].reshape(2 * p, SSD_D_STATE)
                    st_scr[sb, d, :, q * LANES:(q + 1) * LANES] = blk.T

    def chunk_body(i, carry, first, zero_state=False):
        for sb, d in [(sb, d) for sb in range(nb) for d in (0, 1)]:
            causal = (row >= col) if d == 0 else (row <= col)
            c = sb * nc + (i if d == 0 else nc - 1 - i)
            r0 = c * CHUNK if isinstance(c, int) else pl.multiple_of(c * CHUNK, CHUNK)
            rows = pl.ds(r0, CHUNK)
            bc = b_ref[rows, :]
            cc = c_ref[rows, :]
            g = lax.dot_general(cc, bc, (((1,), (1,)), ((), ())), preferred_element_type=F32)
            gm = jnp.where(causal, g, 0.0)
            bt = bc.astype(F32).T
            cols = col_ref[rows, :]
            edge = CHUNK - 1 if d == 0 else 0
            if not zero_state:
                y_off = _dot(cc, st_scr[sb, d].astype(BF16))
            spread = _dot(cols, bcast_ref[d])
            xbd_q, m_q, bt_q, e_q = [], [], [], []
            for q in range(PAIRS):
                x = xs_ref[rows, q * LANES:(q + 1) * LANES]
                zero = jnp.zeros_like(x)
                xbd_q.append(jnp.concatenate([jnp.where(lo_half, x, zero), jnp.where(lo_half, zero, x)], axis=0))
                m_h, bt_h, e_h = [], [], []
                for h in (2 * q, 2 * q + 1):
                    j = d * HEADS_PER_GROUP + h
                    cs_l = spread[:, h * LANES:(h + 1) * LANES]
                    csd_s = row_ref[c, pl.ds(j, 1), :]
                    ldt_s = row_ref[c, pl.ds(N_DH + j, 1), :]
                    w_s = row_ref[c, pl.ds(2 * N_DH + j, 1), :]
                    lmat_dt = jnp.exp2(jnp.minimum(cs_l - csd_s, ldt_s))
                    m_h.append((gm * lmat_dt).astype(BF16))
                    bt_h.append((bt * w_s).astype(BF16))
                    e_h.append(cs_l)
                m_q.append(jnp.concatenate(m_h, axis=1))
                bt_q.append(jnp.concatenate(bt_h, axis=1))
                if not zero_state:
                    e_q.append(jnp.exp2(jnp.where(lo_half, e_h[0], e_h[1])))
            y_diag = [_dot(m_q[q], xbd_q[q]) for q in range(PAIRS)]
            s_new = [_dot(bt_q[q], xbd_q[q]) for q in range(PAIRS)]
            for q in range(PAIRS):
                lanes = slice(q * LANES, (q + 1) * LANES)
                y = y_diag[q] if zero_state else y_diag[q] + y_off[:, lanes] * e_q[q]
                if first:
                    y_scr[rows, lanes] = y + dsk_ref[:, lanes] * xs_ref[rows, lanes].astype(F32)
                else:
                    y_ref[rows, lanes] = (y_scr[rows, lanes] + y).astype(y_ref.dtype)
                if zero_state:
                    st_scr[sb, d, :, lanes] = s_new[q]
                else:
                    st_scr[sb, d, :, lanes] = (st_scr[sb, d, :, lanes] * e_q[q][edge:edge + 1, :]
                                               + s_new[q])
        return carry

    assert nc % 2 == 0
    start = 0
    if not has_init:
        chunk_body(0, 0, first=True, zero_state=True)
        start = 1
    lax.fori_loop(start, nc // 2, functools.partial(chunk_body, first=True), 0)
    lax.fori_loop(nc // 2, nc, functools.partial(chunk_body, first=False), 0)

    if want_final:
        for sb in range(nb):
            for d in (0, 1):
                for q in range(PAIRS):
                    lanes = slice(q * LANES, (q + 1) * LANES)
                    fin_ref[sb, d, 2 * q:2 * q + 2] = st_scr[sb, d, :, lanes].T.reshape(2, p, SSD_D_STATE)


def _ssd_scan(xbc, cols_g, rows_g, d_skip, init, batch, seq, *, want_final):
    nc = seq // CHUNK
    width = SSD_GROUPS * GROUP_W
    heads = SSD_GROUPS * HEADS_PER_GROUP
    has_init = init is not None
    bcast = _lane_bcast_matrices()
    nb = SSD_SEQS_PER_STEP
    assert batch % nb == 0
    rows = nb * seq
    in_specs = [
        pl.BlockSpec((rows, GROUP_W), lambda b, g: (b, g)),
        pl.BlockSpec((rows, SSD_D_STATE), lambda b, g: (b, width // SSD_D_STATE + g)),
        pl.BlockSpec((rows, SSD_D_STATE), lambda b, g: (b, width // SSD_D_STATE + SSD_GROUPS + g)),
        pl.BlockSpec((None, rows, LANES), lambda b, g: (g, b, 0)),
        pl.BlockSpec((None, nb * nc, 3 * N_DH, CHUNK), lambda b, g: (g, b, 0, 0)),
        pl.BlockSpec(bcast.shape, lambda b, g: (0, 0, 0)),
        pl.BlockSpec((1, GROUP_W), lambda b, g: (0, g)),
    ]
    args = [xbc, xbc, xbc, cols_g, rows_g, bcast, d_skip]
    state_spec = pl.BlockSpec((nb, 2, HEADS_PER_GROUP, SSD_HEAD_DIM, SSD_D_STATE),
                              lambda b, g: (b, 0, g, 0, 0))
    if has_init:
        in_specs.append(state_spec)
        args.append(init)
    out_specs = [pl.BlockSpec((rows, GROUP_W), lambda b, g: (b, g))]
    out_shape = [jax.ShapeDtypeStruct((batch * seq, width), BF16)]
    if want_final:
        out_specs.append(state_spec)
        out_shape.append(jax.ShapeDtypeStruct((batch, 2, heads, SSD_HEAD_DIM, SSD_D_STATE), F32))
    out = pl.pallas_call(
        functools.partial(_ssd_kernel, nc=nc, nb=nb, has_init=has_init, want_final=want_final),
        grid=(batch // nb, SSD_GROUPS),
        in_specs=in_specs,
        out_specs=out_specs,
        out_shape=out_shape,
        scratch_shapes=[pltpu.VMEM((nb, 2, SSD_D_STATE, GROUP_W), F32), pltpu.VMEM((rows, GROUP_W), F32)],
        compiler_params=_params(("parallel", "parallel"), 48),
        name="ssd_scan",
    )(*args)
    return out if want_final else (out[0], None)


def _trunk(x, pos, batch, seq, per_seq_cond, init, mod5, wts, *, want_final):
    (ln_g, ln_b, fno_w_in, fno_w_out, ssd_w_in, conv_w, conv_b, dt_bias, a_log,
     d_skip, norm_w, ssd_w_out) = wts
    fno_width = fno_w_out.shape[0]
    ssd_width = ssd_w_out.shape[0]
    tm_in, tm_out, tm_gated = 1024, 512, 512

    def cond_row(tm):
        if not per_seq_cond:
            return lambda i: CTX_ROW
        assert seq % tm == 0
        return lambda i: (i * tm) // seq

    tn0 = 2048 if pos is None else 1024
    uz = _ln_in_proj(x, pos, mod5, 0, cond_row(tm_in), fno_w_in, False, tm=tm_in, tn=tn0)
    yg = _fno_core(uz, batch, seq, fno_width, tw=min(fno_width, 8192 * 256 // seq))
    x1 = _out_proj_ln(yg, None, None, fno_w_out, x, pos, mod5, 0, cond_row(tm_out), ln_g, ln_b, tm=tm_out)

    zx, dt_raw = _ln_in_proj(x1, None, mod5, 1, cond_row(tm_in), ssd_w_in, True, tm=tm_in, tn=2048)
    xbc = _conv_silu(zx, conv_w, conv_b, batch, seq, ssd_width, tc=2048)
    cols_g, rows_g = _dt_prep(dt_raw, dt_bias, a_log, n_ch=16)
    y, fin = _ssd_scan(xbc, cols_g, rows_g, d_skip, init, batch, seq, want_final=want_final)
    x2 = _out_proj_ln(y, zx, norm_w, ssd_w_out, x1, None, mod5, 1, cond_row(tm_gated), ln_g, ln_b, tm=tm_gated)
    return x2, fin


def _sincos(pos, dim):
    omega = 1.0 / (10000.0 ** (np.arange(dim // 2, dtype=np.float64) / (dim / 2)))
    ang = pos.astype(np.float64)[:, None] * omega[None, :]
    return np.concatenate([np.sin(ang), np.cos(ang)], axis=-1)


def _grid_pos_embed(n_tokens, dim):
    t = np.arange(n_tokens)
    return np.concatenate([_sincos(t // GRID_W, dim // 2), _sincos(t % GRID_W, dim // 2)], axis=-1)


def kernel(x_prompt, x_sample, state_ssd_ctx, c, c_ctx, w_ada, b_ada, ln_g, ln_b, fno_w_in, fno_w_out,
           ssd_w_in, ssd_conv_w, ssd_conv_b, ssd_dt_bias, ssd_a_log, ssd_d, ssd_norm_w, ssd_w_out):
    batch, seq, d = x_prompt.shape
    dec_batch, dec_seq, _ = x_sample.shape
    ssd_width = ssd_w_out.shape[1]
    conv_dim = ssd_conv_w.shape[2]
    assert dec_batch <= CTX_ROW

    cond = jnp.zeros((COND_ROWS, d), F32).at[:dec_batch].set(c).at[CTX_ROW].set(c_ctx)
    mod = _modulation(cond, w_ada, b_ada)
    mod5 = mod.reshape(DEPTH, COND_ROWS, 3, 1, d)

    assert ssd_w_in.shape[2] == ssd_width + conv_dim + LANES
    wts = (ln_g, ln_b,
           fno_w_in[0].astype(BF16), fno_w_out[0].astype(BF16), ssd_w_in[0].astype(BF16),
           ssd_conv_w[0], ssd_conv_b[0], ssd_dt_bias[0], ssd_a_log[0],
           jnp.repeat(ssd_d[0], SSD_HEAD_DIM).reshape(1, ssd_width),
           ssd_norm_w[0], ssd_w_out[0].astype(BF16))

    y_prompt, fin = _trunk(x_prompt.reshape(batch * seq, d), None, batch, seq,
                           False, None, mod5, wts, want_final=True)
    pos = jnp.asarray(_grid_pos_embed(dec_seq, d), dtype=F32)
    y_sample, _ = _trunk(x_sample.reshape(dec_batch * dec_seq, d), pos, dec_batch, dec_seq,
                         True, state_ssd_ctx[:, 0], mod5, wts, want_final=False)
    return (y_prompt.reshape(batch, seq, d), y_sample.reshape(dec_batch, dec_seq, d),
            fin[:, None])
```

```python
import functools
import math

import numpy as np
import jax
import jax.numpy as jnp
from jax import lax
from jax.experimental import pallas as pl
from jax.experimental.pallas import tpu as pltpu

F32 = jnp.float32
BF16 = jnp.bfloat16

DEPTH = 2
GRID_W = 64
FNO_GROUP_DIM = 256
SSD_HEAD_DIM = 64
SSD_GROUPS = 8
SSD_D_STATE = 128
SSD_CONV_W = 5
CHUNK = 128
DEEPNORM_ALPHA = (2 * DEPTH) ** 0.25
LN_EPS = 1e-5
COND_ROWS = 16
CTX_ROW = 8
LANES = 128
SUBLANES = 8
MIB = 1024 * 1024


def _params(semantics, vmem_mib):
    return pltpu.CompilerParams(dimension_semantics=semantics, vmem_limit_bytes=vmem_mib * MIB)


def _silu(x):
    h = 0.5 * x
    return h + h * jnp.tanh(h)


def _dot(a, b):
    return jnp.dot(a, b, preferred_element_type=F32)


def _mod_kernel(c_ref, w_ref, b_ref, o_ref):
    s = _silu(c_ref[...]).astype(BF16)
    o_ref[...] = _dot(s, w_ref[...].astype(BF16)) + b_ref[...]


def _modulation(cond, w_ada, b_ada):
    depth, d, n = w_ada.shape
    tn = 1536
    return pl.pallas_call(
        _mod_kernel,
        grid=(depth, n // tn),
        in_specs=[
            pl.BlockSpec((COND_ROWS, d), lambda i, j: (0, 0)),
            pl.BlockSpec((None, d, tn), lambda i, j: (i, 0, j)),
            pl.BlockSpec((None, 1, tn), lambda i, j: (i, 0, j)),
        ],
        out_specs=pl.BlockSpec((None, COND_ROWS, tn), lambda i, j: (i, 0, j)),
        out_shape=jax.ShapeDtypeStruct((depth, COND_ROWS, n), F32),
        compiler_params=_params(("parallel", "parallel"), 40),
        name="adaln_mod",
    )(cond, w_ada, b_ada.reshape(depth, 1, n))


LN_ROWS = 256


def _ln_modulate(x, shift, scale):
    mu = jnp.mean(x, axis=-1, keepdims=True)
    xc = x - mu
    var = jnp.mean(xc * xc, axis=-1, keepdims=True)
    return xc * lax.rsqrt(var + LN_EPS) * (1.0 + scale) + shift


def _ln_mm_kernel(*refs, has_pos, has_dt):
    refs = list(refs)
    x_ref = refs.pop(0)
    pos_ref = refs.pop(0) if has_pos else None
    shift_ref, scale_ref, w_ref = refs.pop(0), refs.pop(0), refs.pop(0)
    wdt_ref = refs.pop(0) if has_dt else None
    o_ref = refs.pop(0)
    dt_ref = refs.pop(0) if has_dt else None
    h_scr = refs.pop(0)

    @pl.when(pl.program_id(1) == 0)
    def _():
        piece = min(LN_ROWS, x_ref.shape[0])
        for r in range(0, x_ref.shape[0], piece):
            rows = slice(r, r + piece)
            x = x_ref[rows, :]
            if has_pos:
                x = x + pos_ref[rows, :]
            h = _ln_modulate(x, shift_ref[...], scale_ref[...]).astype(BF16)
            h_scr[rows, :] = h
            if has_dt:
                dt_ref[rows, :] = _dot(h, wdt_ref[...])

    o_ref[...] = _dot(h_scr[...], w_ref[...]).astype(o_ref.dtype)


def _ln_in_proj(x, pos, mod5, layer, cond_row, w, with_dt, *, tm, tn):
    t, d = x.shape
    has_pos, has_dt = pos is not None, with_dt
    n = w.shape[1] - (LANES if has_dt else 0)
    in_specs = [pl.BlockSpec((tm, d), lambda i, j: (i, 0))]
    args = [x]
    if has_pos:
        pos_blocks = pos.shape[0] // tm
        mode = dict(pipeline_mode=pl.Buffered(1)) if pos_blocks == 1 else {}
        in_specs.append(pl.BlockSpec((tm, d), lambda i, j: (i % pos_blocks, 0), **mode))
        args.append(pos)
    for part in (0, 1):
        in_specs.append(pl.BlockSpec((None, None, None, 1, d),
                                     lambda i, j, part=part: (layer, cond_row(i), part, 0, 0)))
        args.append(mod5)
    in_specs.append(pl.BlockSpec((d, tn), lambda i, j: (0, j)))
    args.append(w)
    out_specs = [pl.BlockSpec((tm, tn), lambda i, j: (i, j))]
    out_shape = [jax.ShapeDtypeStruct((t, n), BF16)]
    if has_dt:
        in_specs.append(pl.BlockSpec((d, LANES), lambda i, j: (0, n // LANES)))
        args.append(w)
        out_specs.append(pl.BlockSpec((tm, LANES), lambda i, j: (i, 0)))
        out_shape.append(jax.ShapeDtypeStruct((t, LANES), F32))
    out = pl.pallas_call(
        functools.partial(_ln_mm_kernel, has_pos=has_pos, has_dt=has_dt),
        grid=(t // tm, n // tn),
        in_specs=in_specs,
        out_specs=out_specs,
        out_shape=out_shape,
        scratch_shapes=[pltpu.VMEM((tm, d), BF16)],
        compiler_params=_params(("parallel", "arbitrary"), 56),
        name="ln_in_proj",
    )(*args)
    return out if has_dt else out[0]


DFT_TAIL = 16


def _dft_matrices(seq, group_dim):
    def cos_sin(n):
        k = np.arange(n, dtype=np.int64)
        ang = 2.0 * np.pi * ((k[:, None] * k[None, :]) % n).astype(np.float64) / n
        return np.cos(ang), np.sin(ang)
    half = seq // 2
    cl, sl = cos_sin(seq)
    cc, sc = cos_sin(group_dim)
    norm = 1.0 / math.sqrt(seq * group_dim)
    posm = np.zeros((2 * half + DFT_TAIL, seq))
    posm[:half] = cl[:half] * norm
    posm[half:2 * half] = sl[:half] * norm
    posm[2 * half] = cl[half] * norm
    flip = np.zeros((half, half))
    flip[0, 0] = 1.0
    flip[np.arange(1, half), half - np.arange(1, half)] = 1.0
    as_bf16 = lambda a: jnp.asarray(a, dtype=F32).astype(BF16)
    return as_bf16(posm), as_bf16(np.stack([cc, sc])), as_bf16(flip)


def _fno_kernel(u_ref, z_ref, posm_ref, chan_ref, flip_ref, o_ref):
    seq, tw = u_ref.shape
    half = seq // 2
    cg = FNO_GROUP_DIM
    first_row = lax.broadcasted_iota(jnp.int32, (half, cg), 0) == 0
    groups = [slice(k * cg, (k + 1) * cg) for k in range(tw // cg)]
    ab = _dot(posm_ref[...], u_ref[...]).astype(BF16)
    p = [_dot(ab[:half, cols], chan_ref[0]) for cols in groups]
    q = [_dot(ab[half:2 * half, cols], chan_ref[1]) for cols in groups]
    p_nyq = [_dot(ab[2 * half:, cols], chan_ref[0])[0:1, :] for cols in groups]
    mirrored = jnp.concatenate(
        [jnp.where(first_row, p_nyq[k], p[k] + q[k]).astype(BF16) for k in range(len(groups))], axis=1)
    bottom = _dot(flip_ref[...], mirrored)
    for k, cols in enumerate(groups):
        gate = _silu(z_ref[:, cols].astype(F32))
        o_ref[:half, cols] = ((p[k] - q[k]) * gate[:half]).astype(o_ref.dtype)
        o_ref[half:, cols] = (bottom[:, cols] * gate[half:]).astype(o_ref.dtype)


def _fno_core(uz, batch, seq, width, *, tw):
    consts = _dft_matrices(seq, FNO_GROUP_DIM)
    nw = width // tw
    const_specs = [pl.BlockSpec(c.shape, lambda b, j, nd=c.ndim: (0,) * nd) for c in consts]
    return pl.pallas_call(
        _fno_kernel,
        grid=(batch, nw),
        in_specs=[
            pl.BlockSpec((seq, tw), lambda b, j: (b, j)),
            pl.BlockSpec((seq, tw), lambda b, j: (b, nw + j)),
        ] + const_specs,
        out_specs=pl.BlockSpec((seq, tw), lambda b, j: (b, j)),
        out_shape=jax.ShapeDtypeStruct((batch * seq, width), BF16),
        compiler_params=_params(("parallel", "parallel"), 58),
        name="fno_core",
    )(uz, uz, *consts)


def _out_ln_kernel(*refs, has_pos, gated):
    refs = list(refs)
    a_ref = refs.pop(0)
    z_ref, nw_ref = (refs.pop(0), refs.pop(0)) if gated else (None, None)
    w_ref, x_ref = refs.pop(0), refs.pop(0)
    pos_ref = refs.pop(0) if has_pos else None
    gate_ref, g_ref, b_ref, o_ref = refs

    if gated:
        y = a_ref[...].astype(F32) * _silu(z_ref[...].astype(F32))
        inv_rms = lax.rsqrt(jnp.mean(y * y, axis=-1, keepdims=True) + LN_EPS)
        acc = _dot((y * nw_ref[...]).astype(BF16), w_ref[...]) * inv_rms
    else:
        acc = _dot(a_ref[...], w_ref[...])
    x = x_ref[...]
    if has_pos:
        x = x + pos_ref[...]
    r = DEEPNORM_ALPHA * x + gate_ref[...] * acc
    mu = jnp.mean(r, axis=-1, keepdims=True)
    rc = r - mu
    var = jnp.mean(rc * rc, axis=-1, keepdims=True)
    o_ref[...] = rc * lax.rsqrt(var + LN_EPS) * g_ref[...] + b_ref[...]


def _out_proj_ln(a, z, norm_w, w, x, pos, mod5, layer, cond_row, ln_g, ln_b, *, tm):
    t, kdim = a.shape
    d = w.shape[1]
    has_pos, gated = pos is not None, z is not None
    row_k = pl.BlockSpec((tm, kdim), lambda i: (i, 0))
    row_d = pl.BlockSpec((tm, d), lambda i: (i, 0))
    in_specs, args = [row_k], [a]
    if gated:
        in_specs += [row_k, pl.BlockSpec((1, kdim), lambda i: (0, 0))]
        args += [z, norm_w.reshape(1, kdim)]
    in_specs += [pl.BlockSpec((kdim, d), lambda i: (0, 0), pipeline_mode=pl.Buffered(1)), row_d]
    args += [w, x]
    if has_pos:
        pos_blocks = pos.shape[0] // tm
        in_specs.append(pl.BlockSpec((tm, d), lambda i: (i % pos_blocks, 0)))
        args.append(pos)
    in_specs.append(pl.BlockSpec((None, None, None, 1, d), lambda i: (layer, cond_row(i), 2, 0, 0)))
    args.append(mod5)
    for v in (ln_g, ln_b):
        in_specs.append(pl.BlockSpec((None, 1, d), lambda i: (layer, 0, 0)))
        args.append(v.reshape(v.shape[0], 1, d))
    return pl.pallas_call(
        functools.partial(_out_ln_kernel, has_pos=has_pos, gated=gated),
        grid=(t // tm,),
        in_specs=in_specs,
        out_specs=row_d,
        out_shape=jax.ShapeDtypeStruct((t, d), F32),
        compiler_params=_params(("parallel",), 58),
        name="out_proj_ln",
    )(*args)


CONV_OFFSETS = (-2, -1, 1, 2)
CONV_PAD = 16
CONV_STRIP = 256
CONV_TILE = 64
CONV_BLOCK_ROWS = 1024


def _shift_matrix():
    tile = CONV_TILE
    s = np.zeros((len(CONV_OFFSETS) * tile, tile + 2 * CONV_PAD), np.float32)
    for t, off in enumerate(CONV_OFFSETS):
        s[t * tile + np.arange(tile), CONV_PAD + np.arange(tile) + off] = 1.0
    return jnp.asarray(s, dtype=BF16)


def _conv_kernel(x_ref, s_ref, w_ref, b_ref, o_ref, pad_scr, *, seq):
    tc = x_ref.shape[1]
    tile = CONV_TILE
    padded = seq + 2 * CONV_PAD
    centre = SSD_CONV_W // 2

    def rows_of(ref, k, lanes):
        return jnp.tile(ref[k * SUBLANES:(k + 1) * SUBLANES, lanes], (tile // SUBLANES, 1))

    for s in range(x_ref.shape[0] // seq):
        base = s * padded
        pad_scr[base:base + CONV_PAD, :] = jnp.zeros((CONV_PAD, tc), BF16)
        pad_scr[base + CONV_PAD + seq:base + padded, :] = jnp.zeros((CONV_PAD, tc), BF16)
        pad_scr[base + CONV_PAD:base + CONV_PAD + seq, :] = x_ref[s * seq:(s + 1) * seq, :]
        for i in range(seq // tile):
            for c in range(0, tc, CONV_STRIP):
                lanes = slice(c, c + CONV_STRIP)
                window = pad_scr[base + i * tile:base + (i + 1) * tile + 2 * CONV_PAD, lanes]
                taps = _dot(s_ref[...], window)
                acc = (rows_of(b_ref, 0, lanes)
                       + window[CONV_PAD:CONV_PAD + tile, :].astype(F32) * rows_of(w_ref, centre, lanes))
                for t, off in enumerate(CONV_OFFSETS):
                    acc = acc + taps[t * tile:(t + 1) * tile, :] * rows_of(w_ref, centre + off, lanes)
                o_ref[s * seq + i * tile:s * seq + (i + 1) * tile, lanes] = _silu(acc).astype(o_ref.dtype)


def _conv_silu(zx, conv_w, conv_b, batch, seq, col0, *, tc):
    conv_dim = conv_w.shape[1]
    assert col0 % tc == 0 and conv_dim % tc == 0
    c0 = col0 // tc
    w_rows = jnp.repeat(conv_w, SUBLANES, axis=0)
    b_rows = jnp.broadcast_to(conv_b.reshape(1, conv_dim), (SUBLANES, conv_dim))
    shift = _shift_matrix()
    nb = max(1, CONV_BLOCK_ROWS // seq)
    assert batch % nb == 0
    rows = nb * seq
    return pl.pallas_call(
        functools.partial(_conv_kernel, seq=seq),
        grid=(batch // nb, conv_dim // tc),
        in_specs=[pl.BlockSpec((rows, tc), lambda b, j: (b, c0 + j)),
                  pl.BlockSpec(shift.shape, lambda b, j: (0, 0)),
                  pl.BlockSpec((SSD_CONV_W * SUBLANES, tc), lambda b, j: (0, j)),
                  pl.BlockSpec((SUBLANES, tc), lambda b, j: (0, j))],
        out_specs=pl.BlockSpec((rows, tc), lambda b, j: (b, j)),
        out_shape=jax.ShapeDtypeStruct((batch * seq, conv_dim), BF16),
        scratch_shapes=[pltpu.VMEM((nb * (seq + 2 * CONV_PAD), tc), BF16)],
        compiler_params=_params(("parallel", "parallel"), 48),
        name="ssd_conv_silu",
    )(zx, shift, w_rows, b_rows)


def _split3(x):
    hi = x.astype(BF16)
    r = x - hi.astype(F32)
    mid = r.astype(BF16)
    lo = (r - mid.astype(F32)).astype(BF16)
    return hi, mid, lo


HEADS_PER_GROUP = 8
PAIRS = HEADS_PER_GROUP // 2
GROUP_W = HEADS_PER_GROUP * SSD_HEAD_DIM
N_DH = 2 * HEADS_PER_GROUP
SPLIT = 3
LOG2E = 1.4426950408889634


def _group_select_matrix():
    sel = np.zeros((SPLIT * LANES, SSD_GROUPS * LANES), np.float32)
    half = LANES // 2
    for g in range(SSD_GROUPS):
        for j in range(N_DH):
            d, h = divmod(j, HEADS_PER_GROUP)
            src = d * half + g * HEADS_PER_GROUP + h
            for p in range(SPLIT):
                sel[p * LANES + src, g * LANES + SPLIT * j + p] = 1.0
    return jnp.asarray(sel, dtype=BF16)


def _lane_bcast_matrices():
    e = np.zeros((2, LANES, HEADS_PER_GROUP * LANES), np.float32)
    for d in range(2):
        for h in range(HEADS_PER_GROUP):
            j = d * HEADS_PER_GROUP + h
            e[d, SPLIT * j:SPLIT * (j + 1), h * LANES:(h + 1) * LANES] = 1.0
    return jnp.asarray(e, dtype=BF16)


def _dt_kernel(raw_ref, bias_ref, alog_ref, sel_ref, col_ref, row_ref):
    n_ch = row_ref.shape[1]
    half = LANES // 2
    row = lax.broadcasted_iota(jnp.int32, (CHUNK, CHUNK), 0)
    col = lax.broadcasted_iota(jnp.int32, (CHUNK, CHUNK), 1)
    tri_f = jnp.where(row >= col, 1.0, 0.0).astype(BF16)
    tri_b = jnp.where(row <= col, 1.0, 0.0).astype(BF16)
    is_fwd = col < half
    a2 = -jnp.exp(alog_ref[...]) * LOG2E
    for k in range(n_ch):
        rows = slice(k * CHUNK, (k + 1) * CHUNK)
        x = raw_ref[rows, :] + bias_ref[...]
        dt = jnp.maximum(x, 0.0) + jnp.log1p(jnp.exp(-jnp.abs(x)))
        parts = _split3(dt * a2)
        cs = jnp.where(is_fwd, sum(_dot(tri_f, p) for p in parts), sum(_dot(tri_b, p) for p in parts))
        tot = jnp.where(is_fwd[:1], cs[CHUNK - 1:CHUNK, :], cs[0:1, :])
        w = dt * jnp.exp2(tot - cs)
        pieces = jnp.concatenate(_split3(cs), axis=1)
        grouped = _dot(pieces, sel_ref[...]).astype(BF16)
        log_dt = jnp.log2(dt)
        transposed = ((cs - log_dt).T, log_dt.T, w.T)
        for g in range(SSD_GROUPS):
            col_ref[g, rows, :] = grouped[:, g * LANES:(g + 1) * LANES]
            for n, v in enumerate(transposed):
                for d in (0, 1):
                    src = d * half + g * HEADS_PER_GROUP
                    dst = n * N_DH + d * HEADS_PER_GROUP
                    row_ref[g, k, dst:dst + HEADS_PER_GROUP, :] = v[src:src + HEADS_PER_GROUP, :]


def _dt_prep(dt_raw, dt_bias, a_log, *, n_ch):
    t = dt_raw.shape[0]
    tm = n_ch * CHUNK
    sel = _group_select_matrix()
    vec_spec = pl.BlockSpec((1, LANES), lambda i: (0, 0))
    return pl.pallas_call(
        _dt_kernel,
        grid=(t // tm,),
        in_specs=[pl.BlockSpec((tm, LANES), lambda i: (i, 0)), vec_spec, vec_spec,
                  pl.BlockSpec(sel.shape, lambda i: (0, 0))],
        out_specs=[pl.BlockSpec((SSD_GROUPS, tm, LANES), lambda i: (0, i, 0)),
                   pl.BlockSpec((SSD_GROUPS, n_ch, 3 * N_DH, CHUNK), lambda i: (0, i, 0, 0))],
        out_shape=[jax.ShapeDtypeStruct((SSD_GROUPS, t, LANES), BF16),
                   jax.ShapeDtypeStruct((SSD_GROUPS, t // CHUNK, 3 * N_DH, CHUNK), F32)],
        compiler_params=_params(("parallel",), 32),
        name="ssd_dt_prep",
    )(dt_raw, dt_bias.reshape(1, LANES), a_log.reshape(1, LANES), sel)


SSD_SEQS_PER_STEP = 4


def _ssd_kernel(*refs, nc, nb, has_init, want_final):
    refs = list(refs)
    xs_ref, b_ref, c_ref, col_ref, row_ref, bcast_ref, dsk_ref = refs[:7]
    del refs[:7]
    init_ref = refs.pop(0) if has_init else None
    y_ref = refs.pop(0)
    fin_ref = refs.pop(0) if want_final else None
    st_scr, y_scr = refs

    p = SSD_HEAD_DIM
    row = lax.broadcasted_iota(jnp.int32, (CHUNK, CHUNK), 0)
    col = lax.broadcasted_iota(jnp.int32, (CHUNK, CHUNK), 1)
    lo_half = col < p

    if has_init:
        for sb in range(nb):
            for d in (0, 1):
                for q in range(PAIRS):
                    blk = init_ref[sb, d, 2 * q:2 * q + 2].reshape(2 * p, SSD_D_STATE)
                    st_scr[sb, d, :, q * LANES:(q + 1) * LANES] = blk.T

    def chunk_body(i, carry, first, zero_state=False):
        for sb, d in [(sb, d) for sb in range(nb) for d in (0, 1)]:
            causal = (row >= col) if d == 0 else (row <= col)
            c = sb * nc + (i if d == 0 else nc - 1 - i)
            r0 = c * CHUNK if isinstance(c, int) else pl.multiple_of(c * CHUNK, CHUNK)
            rows = pl.ds(r0, CHUNK)
            bc = b_ref[rows, :]
            cc = c_ref[rows, :]
            g = lax.dot_general(cc, bc, (((1,), (1,)), ((), ())), preferred_element_type=F32)
            gm = jnp.where(causal, g, 0.0)
            bt = bc.astype(F32).T
            cols = col_ref[rows, :]
            edge = CHUNK - 1 if d == 0 else 0
            if not zero_state:
                y_off = _dot(cc, st_scr[sb, d].astype(BF16))
            spread = _dot(cols, bcast_ref[d])
            xbd_q, m_q, bt_q, e_q = [], [], [], []
            for q in range(PAIRS):
                x = xs_ref[rows, q * LANES:(q + 1) * LANES]
                zero = jnp.zeros_like(x)
                xbd_q.append(jnp.concatenate([jnp.where(lo_half, x, zero), jnp.where(lo_half, zero, x)], axis=0))
                m_h, bt_h, e_h = [], [], []
                for h in (2 * q, 2 * q + 1):
                    j = d * HEADS_PER_GROUP + h
                    cs_l = spread[:, h * LANES:(h + 1) * LANES]
                    csd_s = row_ref[c, pl.ds(j, 1), :]
                    ldt_s = row_ref[c, pl.ds(N_DH + j, 1), :]
                    w_s = row_ref[c, pl.ds(2 * N_DH + j, 1), :]
                    lmat_dt = jnp.exp2(jnp.minimum(cs_l - csd_s, ldt_s))
                    m_h.append((gm * lmat_dt).astype(BF16))
                    bt_h.append((bt * w_s).astype(BF16))
                    e_h.append(cs_l)
                m_q.append(jnp.concatenate(m_h, axis=1))
                bt_q.append(jnp.concatenate(bt_h, axis=1))
                if not zero_state:
                    e_q.append(jnp.exp2(jnp.where(lo_half, e_h[0], e_h[1])))
            y_diag = [_dot(m_q[q], xbd_q[q]) for q in range(PAIRS)]
            s_new = [_dot(bt_q[q], xbd_q[q]) for q in range(PAIRS)]
            for q in range(PAIRS):
                lanes = slice(q * LANES, (q + 1) * LANES)
                y = y_diag[q] if zero_state else y_diag[q] + y_off[:, lanes] * e_q[q]
                if first:
                    y_scr[rows, lanes] = y + dsk_ref[:, lanes] * xs_ref[rows, lanes].astype(F32)
                else:
                    y_ref[rows, lanes] = (y_scr[rows, lanes] + y).astype(y_ref.dtype)
                if zero_state:
                    st_scr[sb, d, :, lanes] = s_new[q]
                else:
                    st_scr[sb, d, :, lanes] = (st_scr[sb, d, :, lanes] * e_q[q][edge:edge + 1, :]
                                               + s_new[q])
        return carry

    assert nc % 2 == 0
    start = 0
    if not has_init:
        chunk_body(0, 0, first=True, zero_state=True)
        start = 1
    lax.fori_loop(start, nc // 2, functools.partial(chunk_body, first=True), 0)
    lax.fori_loop(nc // 2, nc, functools.partial(chunk_body, first=False), 0)

    if want_final:
        for sb in range(nb):
            for d in (0, 1):
                for q in range(PAIRS):
                    lanes = slice(q * LANES, (q + 1) * LANES)
                    fin_ref[sb, d, 2 * q:2 * q + 2] = st_scr[sb, d, :, lanes].T.reshape(2, p, SSD_D_STATE)


def _ssd_scan(xbc, cols_g, rows_g, d_skip, init, batch, seq, *, want_final):
    nc = seq // CHUNK
    width = SSD_GROUPS * GROUP_W
    heads = SSD_GROUPS * HEADS_PER_GROUP
    has_init = init is not None
    bcast = _lane_bcast_matrices()
    nb = SSD_SEQS_PER_STEP * (2 if nc <= 2 else 1)
    assert batch % nb == 0
    rows = nb * seq
    in_specs = [
        pl.BlockSpec((rows, GROUP_W), lambda b, g: (b, g)),
        pl.BlockSpec((rows, SSD_D_STATE), lambda b, g: (b, width // SSD_D_STATE + g)),
        pl.BlockSpec((rows, SSD_D_STATE), lambda b, g: (b, width // SSD_D_STATE + SSD_GROUPS + g)),
        pl.BlockSpec((None, rows, LANES), lambda b, g: (g, b, 0)),
        pl.BlockSpec((None, nb * nc, 3 * N_DH, CHUNK), lambda b, g: (g, b, 0, 0)),
        pl.BlockSpec(bcast.shape, lambda b, g: (0, 0, 0)),
        pl.BlockSpec((1, GROUP_W), lambda b, g: (0, g)),
    ]
    args = [xbc, xbc, xbc, cols_g, rows_g, bcast, d_skip]
    state_spec = pl.BlockSpec((nb, 2, HEADS_PER_GROUP, SSD_HEAD_DIM, SSD_D_STATE),
                              lambda b, g: (b, 0, g, 0, 0))
    if has_init:
        in_specs.append(state_spec)
        args.append(init)
    out_specs = [pl.BlockSpec((rows, GROUP_W), lambda b, g: (b, g))]
    out_shape = [jax.ShapeDtypeStruct((batch * seq, width), BF16)]
    if want_final:
        out_specs.append(state_spec)
        out_shape.append(jax.ShapeDtypeStruct((batch, 2, heads, SSD_HEAD_DIM, SSD_D_STATE), F32))
    out = pl.pallas_call(
        functools.partial(_ssd_kernel, nc=nc, nb=nb, has_init=has_init, want_final=want_final),
        grid=(batch // nb, SSD_GROUPS),
        in_specs=in_specs,
        out_specs=out_specs,
        out_shape=out_shape,
        scratch_shapes=[pltpu.VMEM((nb, 2, SSD_D_STATE, GROUP_W), F32), pltpu.VMEM((rows, GROUP_W), F32)],
        compiler_params=_params(("parallel", "parallel"), 48),
        name="ssd_scan",
    )(*args)
    return out if want_final else (out[0], None)


def _trunk(x, pos, batch, seq, per_seq_cond, init, mod5, wts, *, want_final):
    (ln_g, ln_b, fno_w_in, fno_w_out, ssd_w_in, conv_w, conv_b, dt_bias, a_log,
     d_skip, norm_w, ssd_w_out) = wts
    fno_width = fno_w_out.shape[0]
    ssd_width = ssd_w_out.shape[0]
    tm_in, tm_out, tm_gated = 1024, 512, 512

    def cond_row(tm):
        if not per_seq_cond:
            return lambda i: CTX_ROW
        assert seq % tm == 0
        return lambda i: (i * tm) // seq

    tn0 = 2048 if pos is None else 1024
    uz = _ln_in_proj(x, pos, mod5, 0, cond_row(tm_in), fno_w_in, False, tm=tm_in, tn=tn0)
    yg = _fno_core(uz, batch, seq, fno_width, tw=min(fno_width, 8192 * 256 // seq))
    x1 = _out_proj_ln(yg, None, None, fno_w_out, x, pos, mod5, 0, cond_row(tm_out), ln_g, ln_b, tm=tm_out)

    zx, dt_raw = _ln_in_proj(x1, None, mod5, 1, cond_row(tm_in), ssd_w_in, True, tm=tm_in, tn=2048)
    xbc = _conv_silu(zx, conv_w, conv_b, batch, seq, ssd_width, tc=2048)
    cols_g, rows_g = _dt_prep(dt_raw, dt_bias, a_log, n_ch=16)
    y, fin = _ssd_scan(xbc, cols_g, rows_g, d_skip, init, batch, seq, want_final=want_final)
    x2 = _out_proj_ln(y, zx, norm_w, ssd_w_out, x1, None, mod5, 1, cond_row(tm_gated), ln_g, ln_b, tm=tm_gated)
    return x2, fin


def _sincos(pos, dim):
    omega = 1.0 / (10000.0 ** (np.arange(dim // 2, dtype=np.float64) / (dim / 2)))
    ang = pos.astype(np.float64)[:, None] * omega[None, :]
    return np.concatenate([np.sin(ang), np.cos(ang)], axis=-1)


def _grid_pos_embed(n_tokens, dim):
    t = np.arange(n_tokens)
    return np.concatenate([_sincos(t // GRID_W, dim // 2), _sincos(t % GRID_W, dim // 2)], axis=-1)


def kernel(x_prompt, x_sample, state_ssd_ctx, c, c_ctx, w_ada, b_ada, ln_g, ln_b, fno_w_in, fno_w_out,
           ssd_w_in, ssd_conv_w, ssd_conv_b, ssd_dt_bias, ssd_a_log, ssd_d, ssd_norm_w, ssd_w_out):
    batch, seq, d = x_prompt.shape
    dec_batch, dec_seq, _ = x_sample.shape
    ssd_width = ssd_w_out.shape[1]
    conv_dim = ssd_conv_w.shape[2]
    assert dec_batch <= CTX_ROW

    cond = jnp.zeros((COND_ROWS, d), F32).at[:dec_batch].set(c).at[CTX_ROW].set(c_ctx)
    mod = _modulation(cond, w_ada, b_ada)
    mod5 = mod.reshape(DEPTH, COND_ROWS, 3, 1, d)

    assert ssd_w_in.shape[2] == ssd_width + conv_dim + LANES
    wts = (ln_g, ln_b,
           fno_w_in[0].astype(BF16), fno_w_out[0].astype(BF16), ssd_w_in[0].astype(BF16),
           ssd_conv_w[0], ssd_conv_b[0], ssd_dt_bias[0], ssd_a_log[0],
           jnp.repeat(ssd_d[0], SSD_HEAD_DIM).reshape(1, ssd_width),
           ssd_norm_w[0], ssd_w_out[0].astype(BF16))

    y_prompt, fin = _trunk(x_prompt.reshape(batch * seq, d), None, batch, seq,
                           False, None, mod5, wts, want_final=True)
    pos = jnp.asarray(_grid_pos_embed(dec_seq, d), dtype=F32)
    y_sample, _ = _trunk(x_sample.reshape(dec_batch * dec_seq, d), pos, dec_batch, dec_seq,
                         True, state_ssd_ctx[:, 0], mod5, wts, want_final=False)
    return (y_prompt.reshape(batch, seq, d), y_sample.reshape(dec_batch, dec_seq, d),
            fin[:, None])
```

```python
import functools
import math

import numpy as np
import jax
import jax.numpy as jnp
from jax import lax
from jax.experimental import pallas as pl
from jax.experimental.pallas import tpu as pltpu

F32 = jnp.float32
BF16 = jnp.bfloat16

DEPTH = 2
GRID_W = 64
FNO_GROUP_DIM = 256
SSD_HEAD_DIM = 64
SSD_GROUPS = 8
SSD_D_STATE = 128
SSD_CONV_W = 5
CHUNK = 128
DEEPNORM_ALPHA = (2 * DEPTH) ** 0.25
LN_EPS = 1e-5
COND_ROWS = 16
CTX_ROW = 8
LANES = 128
SUBLANES = 8
MIB = 1024 * 1024


def _params(semantics, vmem_mib):
    return pltpu.CompilerParams(dimension_semantics=semantics, vmem_limit_bytes=vmem_mib * MIB)


def _silu(x):
    h = 0.5 * x
    return h + h * jnp.tanh(h)


def _dot(a, b):
    return jnp.dot(a, b, preferred_element_type=F32)


def _mod_kernel(c_ref, w_ref, b_ref, o_ref):
    s = _silu(c_ref[...]).astype(BF16)
    o_ref[...] = _dot(s, w_ref[...].astype(BF16)) + b_ref[...]


def _modulation(cond, w_ada, b_ada):
    depth, d, n = w_ada.shape
    tn = 1536
    return pl.pallas_call(
        _mod_kernel,
        grid=(depth, n // tn),
        in_specs=[
            pl.BlockSpec((COND_ROWS, d), lambda i, j: (0, 0)),
            pl.BlockSpec((None, d, tn), lambda i, j: (i, 0, j)),
            pl.BlockSpec((None, 1, tn), lambda i, j: (i, 0, j)),
        ],
        out_specs=pl.BlockSpec((None, COND_ROWS, tn), lambda i, j: (i, 0, j)),
        out_shape=jax.ShapeDtypeStruct((depth, COND_ROWS, n), F32),
        compiler_params=_params(("parallel", "parallel"), 40),
        name="adaln_mod",
    )(cond, w_ada, b_ada.reshape(depth, 1, n))


LN_ROWS = 256


def _ln_modulate(x, shift, scale):
    mu = jnp.mean(x, axis=-1, keepdims=True)
    xc = x - mu
    var = jnp.mean(xc * xc, axis=-1, keepdims=True)
    return xc * lax.rsqrt(var + LN_EPS) * (1.0 + scale) + shift


def _ln_mm_kernel(*refs, has_pos, has_dt):
    refs = list(refs)
    x_ref = refs.pop(0)
    pos_ref = refs.pop(0) if has_pos else None
    shift_ref, scale_ref, w_ref = refs.pop(0), refs.pop(0), refs.pop(0)
    wdt_ref = refs.pop(0) if has_dt else None
    o_ref = refs.pop(0)
    dt_ref = refs.pop(0) if has_dt else None
    h_scr = refs.pop(0)

    @pl.when(pl.program_id(1) == 0)
    def _():
        piece = min(LN_ROWS, x_ref.shape[0])
        for r in range(0, x_ref.shape[0], piece):
            rows = slice(r, r + piece)
            x = x_ref[rows, :]
            if has_pos:
                x = x + pos_ref[rows, :]
            h = _ln_modulate(x, shift_ref[...], scale_ref[...]).astype(BF16)
            h_scr[rows, :] = h
            if has_dt:
                dt_ref[rows, :] = _dot(h, wdt_ref[...])

    o_ref[...] = _dot(h_scr[...], w_ref[...]).astype(o_ref.dtype)


def _ln_in_proj(x, pos, mod5, layer, cond_row, w, with_dt, *, tm, tn):
    t, d = x.shape
    has_pos, has_dt = pos is not None, with_dt
    n = w.shape[1] - (LANES if has_dt else 0)
    in_specs = [pl.BlockSpec((tm, d), lambda i, j: (i, 0))]
    args = [x]
    if has_pos:
        pos_blocks = pos.shape[0] // tm
        mode = dict(pipeline_mode=pl.Buffered(1)) if pos_blocks == 1 else {}
        in_specs.append(pl.BlockSpec((tm, d), lambda i, j: (i % pos_blocks, 0), **mode))
        args.append(pos)
    for part in (0, 1):
        in_specs.append(pl.BlockSpec((None, None, None, 1, d),
                                     lambda i, j, part=part: (layer, cond_row(i), part, 0, 0)))
        args.append(mod5)
    in_specs.append(pl.BlockSpec((d, tn), lambda i, j: (0, j)))
    args.append(w)
    out_specs = [pl.BlockSpec((tm, tn), lambda i, j: (i, j))]
    out_shape = [jax.ShapeDtypeStruct((t, n), BF16)]
    if has_dt:
        in_specs.append(pl.BlockSpec((d, LANES), lambda i, j: (0, n // LANES)))
        args.append(w)
        out_specs.append(pl.BlockSpec((tm, LANES), lambda i, j: (i, 0)))
        out_shape.append(jax.ShapeDtypeStruct((t, LANES), F32))
    out = pl.pallas_call(
        functools.partial(_ln_mm_kernel, has_pos=has_pos, has_dt=has_dt),
        grid=(t // tm, n // tn),
        in_specs=in_specs,
        out_specs=out_specs,
        out_shape=out_shape,
        scratch_shapes=[pltpu.VMEM((tm, d), BF16)],
        compiler_params=_params(("parallel", "arbitrary"), 56),
        name="ln_in_proj",
    )(*args)
    return out if has_dt else out[0]


DFT_TAIL = 16


def _dft_matrices(seq, group_dim):
    def cos_sin(n):
        k = np.arange(n, dtype=np.int64)
        ang = 2.0 * np.pi * ((k[:, None] * k[None, :]) % n).astype(np.float64) / n
        return np.cos(ang), np.sin(ang)
    half = seq // 2
    cl, sl = cos_sin(seq)
    cc, sc = cos_sin(group_dim)
    norm = 1.0 / math.sqrt(seq * group_dim)
    posm = np.zeros((2 * half + DFT_TAIL, seq))
    posm[:half] = cl[:half] * norm
    posm[half:2 * half] = sl[:half] * norm
    posm[2 * half] = cl[half] * norm
    flip = np.zeros((half, half))
    flip[0, 0] = 1.0
    flip[np.arange(1, half), half - np.arange(1, half)] = 1.0
    as_bf16 = lambda a: jnp.asarray(a, dtype=F32).astype(BF16)
    return as_bf16(posm), as_bf16(np.stack([cc, sc])), as_bf16(flip)


def _fno_kernel(u_ref, z_ref, posm_ref, chan_ref, flip_ref, o_ref):
    seq, tw = u_ref.shape
    half = seq // 2
    cg = FNO_GROUP_DIM
    first_row = lax.broadcasted_iota(jnp.int32, (half, cg), 0) == 0
    groups = [slice(k * cg, (k + 1) * cg) for k in range(tw // cg)]
    ab = _dot(posm_ref[...], u_ref[...]).astype(BF16)
    p = [_dot(ab[:half, cols], chan_ref[0]) for cols in groups]
    q = [_dot(ab[half:2 * half, cols], chan_ref[1]) for cols in groups]
    p_nyq = [_dot(ab[2 * half:, cols], chan_ref[0])[0:1, :] for cols in groups]
    mirrored = jnp.concatenate(
        [jnp.where(first_row, p_nyq[k], p[k] + q[k]).astype(BF16) for k in range(len(groups))], axis=1)
    bottom = _dot(flip_ref[...], mirrored)
    for k, cols in enumerate(groups):
        gate = _silu(z_ref[:, cols].astype(F32))
        o_ref[:half, cols] = ((p[k] - q[k]) * gate[:half]).astype(o_ref.dtype)
        o_ref[half:, cols] = (bottom[:, cols] * gate[half:]).astype(o_ref.dtype)


def _fno_core(uz, batch, seq, width, *, tw):
    consts = _dft_matrices(seq, FNO_GROUP_DIM)
    nw = width // tw
    const_specs = [pl.BlockSpec(c.shape, lambda b, j, nd=c.ndim: (0,) * nd) for c in consts]
    return pl.pallas_call(
        _fno_kernel,
        grid=(batch, nw),
        in_specs=[
            pl.BlockSpec((seq, tw), lambda b, j: (b, j)),
            pl.BlockSpec((seq, tw), lambda b, j: (b, nw + j)),
        ] + const_specs,
        out_specs=pl.BlockSpec((seq, tw), lambda b, j: (b, j)),
        out_shape=jax.ShapeDtypeStruct((batch * seq, width), BF16),
        compiler_params=_params(("parallel", "parallel"), 58),
        name="fno_core",
    )(uz, uz, *consts)


def _out_ln_kernel(*refs, has_pos, gated):
    refs = list(refs)
    a_ref = refs.pop(0)
    z_ref, nw_ref = (refs.pop(0), refs.pop(0)) if gated else (None, None)
    w_ref, x_ref = refs.pop(0), refs.pop(0)
    pos_ref = refs.pop(0) if has_pos else None
    gate_ref, g_ref, b_ref, o_ref = refs

    if gated:
        y = a_ref[...].astype(F32) * _silu(z_ref[...].astype(F32))
        inv_rms = lax.rsqrt(jnp.mean(y * y, axis=-1, keepdims=True) + LN_EPS)
        acc = _dot((y * nw_ref[...]).astype(BF16), w_ref[...]) * inv_rms
    else:
        acc = _dot(a_ref[...], w_ref[...])
    x = x_ref[...]
    if has_pos:
        x = x + pos_ref[...]
    r = DEEPNORM_ALPHA * x + gate_ref[...] * acc
    mu = jnp.mean(r, axis=-1, keepdims=True)
    rc = r - mu
    var = jnp.mean(rc * rc, axis=-1, keepdims=True)
    o_ref[...] = rc * lax.rsqrt(var + LN_EPS) * g_ref[...] + b_ref[...]


def _out_proj_ln(a, z, norm_w, w, x, pos, mod5, layer, cond_row, ln_g, ln_b, *, tm):
    t, kdim = a.shape
    d = w.shape[1]
    has_pos, gated = pos is not None, z is not None
    row_k = pl.BlockSpec((tm, kdim), lambda i: (i, 0))
    row_d = pl.BlockSpec((tm, d), lambda i: (i, 0))
    in_specs, args = [row_k], [a]
    if gated:
        in_specs += [row_k, pl.BlockSpec((1, kdim), lambda i: (0, 0))]
        args += [z, norm_w.reshape(1, kdim)]
    in_specs += [pl.BlockSpec((kdim, d), lambda i: (0, 0), pipeline_mode=pl.Buffered(1)), row_d]
    args += [w, x]
    if has_pos:
        pos_blocks = pos.shape[0] // tm
        in_specs.append(pl.BlockSpec((tm, d), lambda i: (i % pos_blocks, 0)))
        args.append(pos)
    in_specs.append(pl.BlockSpec((None, None, None, 1, d), lambda i: (layer, cond_row(i), 2, 0, 0)))
    args.append(mod5)
    for v in (ln_g, ln_b):
        in_specs.append(pl.BlockSpec((None, 1, d), lambda i: (layer, 0, 0)))
        args.append(v.reshape(v.shape[0], 1, d))
    return pl.pallas_call(
        functools.partial(_out_ln_kernel, has_pos=has_pos, gated=gated),
        grid=(t // tm,),
        in_specs=in_specs,
        out_specs=row_d,
        out_shape=jax.ShapeDtypeStruct((t, d), F32),
        compiler_params=_params(("parallel",), 58),
        name="out_proj_ln",
    )(*args)


CONV_OFFSETS = (-2, -1, 1, 2)
CONV_PAD = 16
CONV_STRIP = 256
CONV_TILE = 64
CONV_BLOCK_ROWS = 2048


def _shift_matrix():
    tile = CONV_TILE
    s = np.zeros((len(CONV_OFFSETS) * tile, tile + 2 * CONV_PAD), np.float32)
    for t, off in enumerate(CONV_OFFSETS):
        s[t * tile + np.arange(tile), CONV_PAD + np.arange(tile) + off] = 1.0
    return jnp.asarray(s, dtype=BF16)


def _conv_kernel(x_ref, s_ref, w_ref, b_ref, o_ref, pad_scr, *, seq):
    tc = x_ref.shape[1]
    tile = CONV_TILE
    padded = seq + 2 * CONV_PAD
    centre = SSD_CONV_W // 2

    def rows_of(ref, k, lanes):
        return jnp.tile(ref[k * SUBLANES:(k + 1) * SUBLANES, lanes], (tile // SUBLANES, 1))

    for s in range(x_ref.shape[0] // seq):
        base = s * padded
        pad_scr[base:base + CONV_PAD, :] = jnp.zeros((CONV_PAD, tc), BF16)
        pad_scr[base + CONV_PAD + seq:base + padded, :] = jnp.zeros((CONV_PAD, tc), BF16)
        pad_scr[base + CONV_PAD:base + CONV_PAD + seq, :] = x_ref[s * seq:(s + 1) * seq, :]
        for i in range(seq // tile):
            for c in range(0, tc, CONV_STRIP):
                lanes = slice(c, c + CONV_STRIP)
                window = pad_scr[base + i * tile:base + (i + 1) * tile + 2 * CONV_PAD, lanes]
                taps = _dot(s_ref[...], window)
                acc = (rows_of(b_ref, 0, lanes)
                       + window[CONV_PAD:CONV_PAD + tile, :].astype(F32) * rows_of(w_ref, centre, lanes))
                for t, off in enumerate(CONV_OFFSETS):
                    acc = acc + taps[t * tile:(t + 1) * tile, :] * rows_of(w_ref, centre + off, lanes)
                o_ref[s * seq + i * tile:s * seq + (i + 1) * tile, lanes] = _silu(acc).astype(o_ref.dtype)


def _conv_silu(zx, conv_w, conv_b, batch, seq, col0, *, tc):
    conv_dim = conv_w.shape[1]
    assert col0 % tc == 0 and conv_dim % tc == 0
    c0 = col0 // tc
    w_rows = jnp.repeat(conv_w, SUBLANES, axis=0)
    b_rows = jnp.broadcast_to(conv_b.reshape(1, conv_dim), (SUBLANES, conv_dim))
    shift = _shift_matrix()
    nb = max(1, CONV_BLOCK_ROWS // seq)
    assert batch % nb == 0
    rows = nb * seq
    return pl.pallas_call(
        functools.partial(_conv_kernel, seq=seq),
        grid=(batch // nb, conv_dim // tc),
        in_specs=[pl.BlockSpec((rows, tc), lambda b, j: (b, c0 + j)),
                  pl.BlockSpec(shift.shape, lambda b, j: (0, 0)),
                  pl.BlockSpec((SSD_CONV_W * SUBLANES, tc), lambda b, j: (0, j)),
                  pl.BlockSpec((SUBLANES, tc), lambda b, j: (0, j))],
        out_specs=pl.BlockSpec((rows, tc), lambda b, j: (b, j)),
        out_shape=jax.ShapeDtypeStruct((batch * seq, conv_dim), BF16),
        scratch_shapes=[pltpu.VMEM((nb * (seq + 2 * CONV_PAD), tc), BF16)],
        compiler_params=_params(("parallel", "parallel"), 48),
        name="ssd_conv_silu",
    )(zx, shift, w_rows, b_rows)


def _split3(x):
    hi = x.astype(BF16)
    r = x - hi.astype(F32)
    mid = r.astype(BF16)
    lo = (r - mid.astype(F32)).astype(BF16)
    return hi, mid, lo


HEADS_PER_GROUP = 8
PAIRS = HEADS_PER_GROUP // 2
GROUP_W = HEADS_PER_GROUP * SSD_HEAD_DIM
N_DH = 2 * HEADS_PER_GROUP
SPLIT = 3
LOG2E = 1.4426950408889634


def _group_select_matrix():
    sel = np.zeros((SPLIT * LANES, SSD_GROUPS * LANES), np.float32)
    half = LANES // 2
    for g in range(SSD_GROUPS):
        for j in range(N_DH):
            d, h = divmod(j, HEADS_PER_GROUP)
            src = d * half + g * HEADS_PER_GROUP + h
            for p in range(SPLIT):
                sel[p * LANES + src, g * LANES + SPLIT * j + p] = 1.0
    return jnp.asarray(sel, dtype=BF16)


def _lane_bcast_matrices():
    e = np.zeros((2, LANES, HEADS_PER_GROUP * LANES), np.float32)
    for d in range(2):
        for h in range(HEADS_PER_GROUP):
            j = d * HEADS_PER_GROUP + h
            e[d, SPLIT * j:SPLIT * (j + 1), h * LANES:(h + 1) * LANES] = 1.0
    return jnp.asarray(e, dtype=BF16)


def _dt_kernel(raw_ref, bias_ref, alog_ref, sel_ref, col_ref, row_ref):
    n_ch = row_ref.shape[1]
    half = LANES // 2
    row = lax.broadcasted_iota(jnp.int32, (CHUNK, CHUNK), 0)
    col = lax.broadcasted_iota(jnp.int32, (CHUNK, CHUNK), 1)
    tri_f = jnp.where(row >= col, 1.0, 0.0).astype(BF16)
    tri_b = jnp.where(row <= col, 1.0, 0.0).astype(BF16)
    is_fwd = col < half
    a2 = -jnp.exp(alog_ref[...]) * LOG2E
    for k in range(n_ch):
        rows = slice(k * CHUNK, (k + 1) * CHUNK)
        x = raw_ref[rows, :] + bias_ref[...]
        dt = jnp.maximum(x, 0.0) + jnp.log1p(jnp.exp(-jnp.abs(x)))
        parts = _split3(dt * a2)
        cs = jnp.where(is_fwd, sum(_dot(tri_f, p) for p in parts), sum(_dot(tri_b, p) for p in parts))
        tot = jnp.where(is_fwd[:1], cs[CHUNK - 1:CHUNK, :], cs[0:1, :])
        w = dt * jnp.exp2(tot - cs)
        pieces = jnp.concatenate(_split3(cs), axis=1)
        grouped = _dot(pieces, sel_ref[...]).astype(BF16)
        log_dt = jnp.log2(dt)
        transposed = ((cs - log_dt).T, log_dt.T, w.T)
        for g in range(SSD_GROUPS):
            col_ref[g, rows, :] = grouped[:, g * LANES:(g + 1) * LANES]
            for n, v in enumerate(transposed):
                for d in (0, 1):
                    src = d * half + g * HEADS_PER_GROUP
                    dst = n * N_DH + d * HEADS_PER_GROUP
                    row_ref[g, k, dst:dst + HEADS_PER_GROUP, :] = v[src:src + HEADS_PER_GROUP, :]


def _dt_prep(dt_raw, dt_bias, a_log, *, n_ch):
    t = dt_raw.shape[0]
    tm = n_ch * CHUNK
    sel = _group_select_matrix()
    vec_spec = pl.BlockSpec((1, LANES), lambda i: (0, 0))
    return pl.pallas_call(
        _dt_kernel,
        grid=(t // tm,),
        in_specs=[pl.BlockSpec((tm, LANES), lambda i: (i, 0)), vec_spec, vec_spec,
                  pl.BlockSpec(sel.shape, lambda i: (0, 0))],
        out_specs=[pl.BlockSpec((SSD_GROUPS, tm, LANES), lambda i: (0, i, 0)),
                   pl.BlockSpec((SSD_GROUPS, n_ch, 3 * N_DH, CHUNK), lambda i: (0, i, 0, 0))],
        out_shape=[jax.ShapeDtypeStruct((SSD_GROUPS, t, LANES), BF16),
                   jax.ShapeDtypeStruct((SSD_GROUPS, t // CHUNK, 3 * N_DH, CHUNK), F32)],
        compiler_params=_params(("parallel",), 32),
        name="ssd_dt_prep",
    )(dt_raw, dt_bias.reshape(1, LANES), a_log.reshape(1, LANES), sel)


SSD_SEQS_PER_STEP = 4


def _ssd_kernel(*refs, nc, nb, has_init, want_final):
    refs = list(refs)
    xs_ref, b_ref, c_ref, col_ref, row_ref, bcast_ref, dsk_ref = refs[:7]
    del refs[:7]
    init_ref = refs.pop(0) if has_init else None
    y_ref = refs.pop(0)
    fin_ref = refs.pop(0) if want_final else None
    st_scr, y_scr = refs

    p = SSD_HEAD_DIM
    row = lax.broadcasted_iota(jnp.int32, (CHUNK, CHUNK), 0)
    col = lax.broadcasted_iota(jnp.int32, (CHUNK, CHUNK), 1)
    lo_half = col < p

    if has_init:
        for sb in range(nb):
            for d in (0, 1):
                for q in range(PAIRS):
                    blk = init_ref[sb, d, 2 * q:2 * q + 2].reshape(2 * p, SSD_D_STATE)
                    st_scr[sb, d, :, q * LANES:(q + 1) * LANES] = blk.T

    def chunk_body(i, carry, first, zero_state=False):
        for sb, d in [(sb, d) for sb in range(nb) for d in (0, 1)]:
            causal = (row >= col) if d == 0 else (row <= col)
            c = sb * nc + (i if d == 0 else nc - 1 - i)
            r0 = c * CHUNK if isinstance(c, int) else pl.multiple_of(c * CHUNK, CHUNK)
            rows = pl.ds(r0, CHUNK)
            bc = b_ref[rows, :]
            cc = c_ref[rows, :]
            g = lax.dot_general(cc, bc, (((1,), (1,)), ((), ())), preferred_element_type=F32)
            gm = jnp.where(causal, g, 0.0)
            bt = bc.astype(F32).T
            cols = col_ref[rows, :]
            edge = CHUNK - 1 if d == 0 else 0
            if not zero_state:
                y_off = _dot(cc, st_scr[sb, d].astype(BF16))
            spread = _dot(cols, bcast_ref[d])
            xbd_q, m_q, bt_q, e_q = [], [], [], []
            for q in range(PAIRS):
                x = xs_ref[rows, q * LANES:(q + 1) * LANES]
                zero = jnp.zeros_like(x)
                xbd_q.append(jnp.concatenate([jnp.where(lo_half, x, zero), jnp.where(lo_half, zero, x)], axis=0))
                m_h, bt_h, e_h = [], [], []
                for h in (2 * q, 2 * q + 1):
                    j = d * HEADS_PER_GROUP + h
                    cs_l = spread[:, h * LANES:(h + 1) * LANES]
                    csd_s = row_ref[c, pl.ds(j, 1), :]
                    ldt_s = row_ref[c, pl.ds(N_DH + j, 1), :]
                    w_s = row_ref[c, pl.ds(2 * N_DH + j, 1), :]
                    lmat_dt = jnp.exp2(jnp.minimum(cs_l - csd_s, ldt_s))
                    m_h.append((gm * lmat_dt).astype(BF16))
                    bt_h.append((bt * w_s).astype(BF16))
                    e_h.append(cs_l)
                m_q.append(jnp.concatenate(m_h, axis=1))
                bt_q.append(jnp.concatenate(bt_h, axis=1))
                if not zero_state:
                    e_q.append(jnp.exp2(jnp.where(lo_half, e_h[0], e_h[1])))
            y_diag = [_dot(m_q[q], xbd_q[q]) for q in range(PAIRS)]
            s_new = [_dot(bt_q[q], xbd_q[q]) for q in range(PAIRS)]
            for q in range(PAIRS):
                lanes = slice(q * LANES, (q + 1) * LANES)
                y = y_diag[q] if zero_state else y_diag[q] + y_off[:, lanes] * e_q[q]
                if first:
                    y_scr[rows, lanes] = y + dsk_ref[:, lanes] * xs_ref[rows, lanes].astype(F32)
                else:
                    y_ref[rows, lanes] = (y_scr[rows, lanes] + y).astype(y_ref.dtype)
                if zero_state:
                    st_scr[sb, d, :, lanes] = s_new[q]
                else:
                    st_scr[sb, d, :, lanes] = (st_scr[sb, d, :, lanes] * e_q[q][edge:edge + 1, :]
                                               + s_new[q])
        return carry

    assert nc % 2 == 0
    start = 0
    if not has_init:
        chunk_body(0, 0, first=True, zero_state=True)
        start = 1
    lax.fori_loop(start, nc // 2, functools.partial(chunk_body, first=True), 0)
    lax.fori_loop(nc // 2, nc, functools.partial(chunk_body, first=False), 0)

    if want_final:
        for sb in range(nb):
            for d in (0, 1):
                for q in range(PAIRS):
                    lanes = slice(q * LANES, (q + 1) * LANES)
                    fin_ref[sb, d, 2 * q:2 * q + 2] = st_scr[sb, d, :, lanes].T.reshape(2, p, SSD_D_STATE)


def _ssd_scan(xbc, cols_g, rows_g, d_skip, init, batch, seq, *, want_final):
    nc = seq // CHUNK
    width = SSD_GROUPS * GROUP_W
    heads = SSD_GROUPS * HEADS_PER_GROUP
    has_init = init is not None
    bcast = _lane_bcast_matrices()
    nb = SSD_SEQS_PER_STEP * (2 if nc <= 2 else 1)
    assert batch % nb == 0
    rows = nb * seq
    in_specs = [
        pl.BlockSpec((rows, GROUP_W), lambda b, g: (b, g)),
        pl.BlockSpec((rows, SSD_D_STATE), lambda b, g: (b, width // SSD_D_STATE + g)),
        pl.BlockSpec((rows, SSD_D_STATE), lambda b, g: (b, width // SSD_D_STATE + SSD_GROUPS + g)),
        pl.BlockSpec((None, rows, LANES), lambda b, g: (g, b, 0)),
        pl.BlockSpec((None, nb * nc, 3 * N_DH, CHUNK), lambda b, g: (g, b, 0, 0)),
        pl.BlockSpec(bcast.shape, lambda b, g: (0, 0, 0)),
        pl.BlockSpec((1, GROUP_W), lambda b, g: (0, g)),
    ]
    args = [xbc, xbc, xbc, cols_g, rows_g, bcast, d_skip]
    state_spec = pl.BlockSpec((nb, 2, HEADS_PER_GROUP, SSD_HEAD_DIM, SSD_D_STATE),
                              lambda b, g: (b, 0, g, 0, 0))
    if has_init:
        in_specs.append(state_spec)
        args.append(init)
    out_specs = [pl.BlockSpec((rows, GROUP_W), lambda b, g: (b, g))]
    out_shape = [jax.ShapeDtypeStruct((batch * seq, width), BF16)]
    if want_final:
        out_specs.append(state_spec)
        out_shape.append(jax.ShapeDtypeStruct((batch, 2, heads, SSD_HEAD_DIM, SSD_D_STATE), F32))
    out = pl.pallas_call(
        functools.partial(_ssd_kernel, nc=nc, nb=nb, has_init=has_init, want_final=want_final),
        grid=(batch // nb, SSD_GROUPS),
        in_specs=in_specs,
        out_specs=out_specs,
        out_shape=out_shape,
        scratch_shapes=[pltpu.VMEM((nb, 2, SSD_D_STATE, GROUP_W), F32), pltpu.VMEM((rows, GROUP_W), F32)],
        compiler_params=_params(("parallel", "parallel"), 48),
        name="ssd_scan",
    )(*args)
    return out if want_final else (out[0], None)


def _trunk(x, pos, batch, seq, per_seq_cond, init, mod5, wts, *, want_final):
    (ln_g, ln_b, fno_w_in, fno_w_out, ssd_w_in, conv_w, conv_b, dt_bias, a_log,
     d_skip, norm_w, ssd_w_out) = wts
    fno_width = fno_w_out.shape[0]
    ssd_width = ssd_w_out.shape[0]
    tm_in, tm_out, tm_gated = 1024, 512, 512

    def cond_row(tm):
        if not per_seq_cond:
            return lambda i: CTX_ROW
        assert seq % tm == 0
        return lambda i: (i * tm) // seq

    tn0 = 2048 if pos is None else 1024
    uz = _ln_in_proj(x, pos, mod5, 0, cond_row(tm_in), fno_w_in, False, tm=tm_in, tn=tn0)
    yg = _fno_core(uz, batch, seq, fno_width, tw=min(fno_width, 8192 * 256 // seq))
    x1 = _out_proj_ln(yg, None, None, fno_w_out, x, pos, mod5, 0, cond_row(tm_out), ln_g, ln_b, tm=tm_out)

    zx, dt_raw = _ln_in_proj(x1, None, mod5, 1, cond_row(tm_in), ssd_w_in, True, tm=tm_in, tn=2048)
    xbc = _conv_silu(zx, conv_w, conv_b, batch, seq, ssd_width, tc=2048)
    cols_g, rows_g = _dt_prep(dt_raw, dt_bias, a_log, n_ch=16)
    y, fin = _ssd_scan(xbc, cols_g, rows_g, d_skip, init, batch, seq, want_final=want_final)
    x2 = _out_proj_ln(y, zx, norm_w, ssd_w_out, x1, None, mod5, 1, cond_row(tm_gated), ln_g, ln_b, tm=tm_gated)
    return x2, fin


def _sincos(pos, dim):
    omega = 1.0 / (10000.0 ** (np.arange(dim // 2, dtype=np.float64) / (dim / 2)))
    ang = pos.astype(np.float64)[:, None] * omega[None, :]
    return np.concatenate([np.sin(ang), np.cos(ang)], axis=-1)


def _grid_pos_embed(n_tokens, dim):
    t = np.arange(n_tokens)
    return np.concatenate([_sincos(t // GRID_W, dim // 2), _sincos(t % GRID_W, dim // 2)], axis=-1)


def kernel(x_prompt, x_sample, state_ssd_ctx, c, c_ctx, w_ada, b_ada, ln_g, ln_b, fno_w_in, fno_w_out,
           ssd_w_in, ssd_conv_w, ssd_conv_b, ssd_dt_bias, ssd_a_log, ssd_d, ssd_norm_w, ssd_w_out):
    batch, seq, d = x_prompt.shape
    dec_batch, dec_seq, _ = x_sample.shape
    ssd_width = ssd_w_out.shape[1]
    conv_dim = ssd_conv_w.shape[2]
    assert dec_batch <= CTX_ROW

    cond = jnp.zeros((COND_ROWS, d), F32).at[:dec_batch].set(c).at[CTX_ROW].set(c_ctx)
    mod = _modulation(cond, w_ada, b_ada)
    mod5 = mod.reshape(DEPTH, COND_ROWS, 3, 1, d)

    assert ssd_w_in.shape[2] == ssd_width + conv_dim + LANES
    wts = (ln_g, ln_b,
           fno_w_in[0].astype(BF16), fno_w_out[0].astype(BF16), ssd_w_in[0].astype(BF16),
           ssd_conv_w[0], ssd_conv_b[0], ssd_dt_bias[0], ssd_a_log[0],
           jnp.repeat(ssd_d[0], SSD_HEAD_DIM).reshape(1, ssd_width),
           ssd_norm_w[0], ssd_w_out[0].astype(BF16))

    y_prompt, fin = _trunk(x_prompt.reshape(batch * seq, d), None, batch, seq,
                           False, None, mod5, wts, want_final=True)
    pos = jnp.asarray(_grid_pos_embed(dec_seq, d), dtype=F32)
    y_sample, _ = _trunk(x_sample.reshape(dec_batch * dec_seq, d), pos, dec_batch, dec_seq,
                         True, state_ssd_ctx[:, 0], mod5, wts, want_final=False)
    return (y_prompt.reshape(batch, seq, d), y_sample.reshape(dec_batch, dec_seq, d),
            fin[:, None])
```
